```python
import math
import jax, jax.numpy as jnp
from jax import lax
import numpy as np

D_MODEL = 1024
BATCH = 1
SEQ = 16384
DEPTH = 4

D_FF = 2816
HEAD_DIM = 64
N_Q_HEADS = 16
N_KV_HEADS = 2
Q_PER_KV = N_Q_HEADS // N_KV_HEADS
ATTN_WIDTH = N_Q_HEADS * HEAD_DIM
KV_WIDTH = N_KV_HEADS * HEAD_DIM
WINDOW = 128
ATTN_BLOCK = 128
SGU_CHUNK = 128
SGU_GROUPS = 8
SGU_GROUP_CH = 128
SGU_WIDTH = SGU_GROUPS * SGU_GROUP_CH
IN_SPLIT_SIZES = (ATTN_WIDTH, KV_WIDTH, KV_WIDTH, SGU_WIDTH, SGU_WIDTH, D_MODEL, D_MODEL)
IN_WIDTH = sum(IN_SPLIT_SIZES)
IN_SPLIT_POINTS = tuple(int(p) for p in np.cumsum(IN_SPLIT_SIZES)[:-1])

RMS_EPS = 1e-6
LN_EPS = 1e-5
MASK_VALUE = -1e30

kernel_name = "hybrid_swa_sink_gmlp_macaron_sandwich"


def rms_norm(x, g):
    x32 = x.astype(jnp.float32)
    y = x32 * lax.rsqrt(jnp.mean(x32 * x32, axis=-1, keepdims=True) + RMS_EPS)
    return (y * g.astype(jnp.float32)).astype(x.dtype)


def layer_norm(x, g, b):
    x32 = x.astype(jnp.float32)
    mu = jnp.mean(x32, axis=-1, keepdims=True)
    xc = x32 - mu
    y = xc * lax.rsqrt(jnp.mean(xc * xc, axis=-1, keepdims=True) + LN_EPS)
    return (y * g.astype(jnp.float32) + b.astype(jnp.float32)).astype(x.dtype)


def swiglu(x, w1, w2):
    g, u = jnp.split(x @ w1, 2, axis=-1)
    return (jax.nn.silu(g) * u) @ w2


def sliding_window_attention(q, k, v, sinks):
    B, S, _ = q.shape
    nb = S // ATTN_BLOCK
    qb = q.reshape(B, nb, ATTN_BLOCK, N_KV_HEADS, Q_PER_KV, HEAD_DIM).astype(jnp.float32)
    kb = k.reshape(B, nb, ATTN_BLOCK, N_KV_HEADS, HEAD_DIM)
    vb = v.reshape(B, nb, ATTN_BLOCK, N_KV_HEADS, HEAD_DIM)
    kpad = jnp.zeros_like(kb[:, :1])
    vpad = jnp.zeros_like(vb[:, :1])
    k2 = jnp.concatenate([jnp.concatenate([kpad, kb[:, :-1]], axis=1), kb], axis=2).astype(jnp.float32)
    v2 = jnp.concatenate([jnp.concatenate([vpad, vb[:, :-1]], axis=1), vb], axis=2).astype(jnp.float32)
    scale = 1.0 / math.sqrt(HEAD_DIM)
    scores = jnp.einsum('bnqgrd,bnkgd->bngrqk', qb, k2) * scale
    qi = jnp.arange(ATTN_BLOCK)[:, None]
    kj = jnp.arange(2 * ATTN_BLOCK)[None, :]
    rel = qi + ATTN_BLOCK - kj
    band = (rel >= 0) & (rel < WINDOW)
    blk = jnp.arange(nb)[:, None, None]
    valid = band[None] & ((blk > 0) | (kj[None] >= ATTN_BLOCK))
    scores = jnp.where(valid[None, :, None, None], scores, MASK_VALUE)
    sink = sinks.astype(jnp.float32).reshape(1, 1, N_KV_HEADS, Q_PER_KV, 1)
    m = jnp.maximum(scores.max(axis=-1), sink)
    p = jnp.exp(scores - m[..., None])
    probs = p / (p.sum(axis=-1) + jnp.exp(sink - m))[..., None]
    out = jnp.einsum('bngrqk,bnkgd->bnqgrd', probs, v2)
    return out.reshape(B, S, ATTN_WIDTH).astype(q.dtype)


def spatial_gating(u, v, ln_g, ln_b, w_s, b_s):
    B, S, _ = u.shape
    nc = S // SGU_CHUNK
    vn = layer_norm(v, ln_g, ln_b).reshape(B, nc, SGU_CHUNK, SGU_GROUPS, SGU_GROUP_CH)
    causal = jnp.tril(jnp.ones((SGU_CHUNK, SGU_CHUNK), dtype=bool))
    w = jnp.where(causal[None], w_s, jnp.zeros_like(w_s))
    s = jnp.einsum('gts,bnsgc->bntgc', w, vn) + b_s.T[None, None, :, :, None]
    return u * s.reshape(B, S, SGU_WIDTH)


def setup_inputs(seed: int = 0) -> dict:
    key = jax.random.key(seed)
    ks = iter(jax.random.split(key, 32))
    f32 = jnp.float32

    def nrm(shape, fan_in, scale=1.0):
        return jax.random.normal(next(ks), shape, f32) * (scale * fan_in ** -0.5)

    def gain(shape):
        return 1.0 + 0.05 * jax.random.normal(next(ks), shape, f32)

    L, D = DEPTH, D_MODEL
    return {
        "x": jax.random.normal(next(ks), (BATCH, SEQ, D), f32),
        "ffn1_pre_g": gain((L, D)),
        "ffn1_w1": nrm((L, D, 2 * D_FF), D),
        "ffn1_w2": nrm((L, D_FF, D), D_FF),
        "ffn1_post_g": gain((L, D)),
        "mix_pre_g": gain((L, D)),
        "w_in": nrm((L, D, IN_WIDTH), D),
        "attn_sinks": 0.5 * jax.random.normal(next(ks), (L, N_Q_HEADS), f32),
        "sgu_ln_g": gain((L, SGU_WIDTH)),
        "sgu_ln_b": 0.02 * jax.random.normal(next(ks), (L, SGU_WIDTH), f32),
        "sgu_w": nrm((L, SGU_GROUPS, SGU_CHUNK, SGU_CHUNK), SGU_CHUNK, 0.5),
        "sgu_b": gain((L, SGU_GROUPS, SGU_CHUNK)),
        "w_attn_branch": nrm((L, ATTN_WIDTH, D), ATTN_WIDTH),
        "w_sgu_branch": nrm((L, SGU_WIDTH, D), SGU_WIDTH),
        "w_out": nrm((L, D, D), D),
        "mix_post_g": gain((L, D)),
        "ffn2_pre_g": gain((L, D)),
        "ffn2_w1": nrm((L, D, 2 * D_FF), D),
        "ffn2_w2": nrm((L, D_FF, D), D_FF),
        "ffn2_post_g": gain((L, D)),
    }


def reference(x, ffn1_pre_g, ffn1_w1, ffn1_w2, ffn1_post_g, mix_pre_g, w_in, attn_sinks,
              sgu_ln_g, sgu_ln_b, sgu_w, sgu_b, w_attn_branch, w_sgu_branch, w_out, mix_post_g,
              ffn2_pre_g, ffn2_w1, ffn2_w2, ffn2_post_g):
    for l in range(DEPTH):
        h = rms_norm(x, ffn1_pre_g[l])
        x = x + 0.5 * rms_norm(swiglu(h, ffn1_w1[l], ffn1_w2[l]), ffn1_post_g[l])

        h = rms_norm(x, mix_pre_g[l])
        z = h @ w_in[l]
        q, k, v, u_s, v_s, g_a, g_b = jnp.split(z, IN_SPLIT_POINTS, axis=-1)
        y_attn = sliding_window_attention(q, k, v, attn_sinks[l])
        y_sgu = spatial_gating(jax.nn.gelu(u_s, approximate=False), jax.nn.gelu(v_s, approximate=False),
                               sgu_ln_g[l], sgu_ln_b[l], sgu_w[l], sgu_b[l])
        merged = (jax.nn.sigmoid(g_a) * (y_attn @ w_attn_branch[l])
                  + jax.nn.sigmoid(g_b) * (y_sgu @ w_sgu_branch[l]))
        x = x + rms_norm(merged @ w_out[l], mix_post_g[l])

        h = rms_norm(x, ffn2_pre_g[l])
        x = x + 0.5 * rms_norm(swiglu(h, ffn2_w1[l], ffn2_w2[l]), ffn2_post_g[l])
    return x
```

```python
import functools
import math

import jax
import jax.numpy as jnp
from jax import lax
from jax.experimental import pallas as pl
from jax.experimental.pallas import tpu as pltpu

D_MODEL = 1024
D_FF = 2816
HEAD_DIM = 64
N_Q_HEADS = 16
N_KV_HEADS = 2
Q_PER_KV = N_Q_HEADS // N_KV_HEADS
ATTN_WIDTH = N_Q_HEADS * HEAD_DIM
KV_WIDTH = N_KV_HEADS * HEAD_DIM
ATTN_BLOCK = 128
SGU_CHUNK = 128
SGU_GROUPS = 8
SGU_GROUP_CH = 128
SGU_WIDTH = SGU_GROUPS * SGU_GROUP_CH
OFF_Q = 0
OFF_KV = ATTN_WIDTH
OFF_U = OFF_KV + 2 * KV_WIDTH
OFF_V = OFF_U + SGU_WIDTH
OFF_GA = OFF_V + SGU_WIDTH
OFF_GB = OFF_GA + D_MODEL
IN_WIDTH = OFF_GB + D_MODEL

RMS_EPS = 1e-6
LN_EPS = 1e-5
MASK_VALUE = -1e30

LANES = 128
MXU_COLS = 256
VMEM_LIMIT_BYTES = 58 * 1024 * 1024

FFN_ROWS = 512
FFN_CHUNK = MXU_COLS
MIX_ROWS = 512

F32 = jnp.float32
BF16 = jnp.bfloat16


def _rms_norm(x, g):
    return x * lax.rsqrt(jnp.mean(x * x, axis=-1, keepdims=True) + RMS_EPS) * g


def _gelu(x):
    return 0.5 * x * (1.0 + lax.erf(x * math.sqrt(0.5)))


def _dot(a, b):
    return jnp.dot(a, b, preferred_element_type=F32)


def _dot_nt(a, b):
    return lax.dot_general(a, b, (((1,), (1,)), ((), ())), preferred_element_type=F32)


def _ffn_kernel(x_ref, pre_g_ref, w1_ref, w2_ref, post_g_ref, o_ref, h_ref, act_ref):
    x = x_ref[...]
    h_ref[...] = _rms_norm(x, pre_g_ref[...]).astype(BF16)
    for c in range(D_FF // FFN_CHUNK):
        lo = c * FFN_CHUNK
        g = _dot(h_ref[...], w1_ref[:, lo:lo + FFN_CHUNK])
        u = _dot(h_ref[...], w1_ref[:, D_FF + lo:D_FF + lo + FFN_CHUNK])
        act_ref[:, lo:lo + FFN_CHUNK] = (jax.nn.silu(g) * u).astype(BF16)
    y = _dot(act_ref[...], w2_ref[...])
    o_ref[...] = x + 0.5 * _rms_norm(y, post_g_ref[...])


def _resident(block_shape, index_map):
    return pl.BlockSpec(block_shape, index_map, pipeline_mode=pl.Buffered(1))


def _ffn(x, pre_g, w1, w2, post_g, layer):
    seq = x.shape[0]
    row = lambda i: (i, 0)
    lay2 = lambda i: (layer, 0, 0)
    return pl.pallas_call(
        _ffn_kernel,
        grid=(seq // FFN_ROWS,),
        in_specs=[
            pl.BlockSpec((FFN_ROWS, D_MODEL), row),
            _resident((None, 1, D_MODEL), lay2),
            _resident((None, D_MODEL, 2 * D_FF), lay2),
            _resident((None, D_FF, D_MODEL), lay2),
            _resident((None, 1, D_MODEL), lay2),
        ],
        out_specs=pl.BlockSpec((FFN_ROWS, D_MODEL), row),
        out_shape=jax.ShapeDtypeStruct(x.shape, x.dtype),
        scratch_shapes=[
            pltpu.VMEM((FFN_ROWS, D_MODEL), BF16),
            pltpu.VMEM((FFN_ROWS, D_FF), BF16),
        ],
        compiler_params=pltpu.CompilerParams(
            dimension_semantics=("arbitrary",), vmem_limit_bytes=VMEM_LIMIT_BYTES),
        name="ffn_half_step",
    )(x, pre_g, w1, w2, post_g)


def _mixer_kernel(sinks_ref, x_ref, pre_g_ref, w_in_ref, ln_g_ref, ln_b_ref, sgu_w_ref, sgu_bt_ref,
                  w_a_ref, w_s_ref, w_o_ref, post_g_ref, o_ref,
                  h_ref, q_ref, kd_ref, vp_ref, u_ref, vn_ref, ga_ref, gb_ref, ya_ref, ys_ref,
                  *, layer):
    step = pl.program_id(0)
    rows = MIX_ROWS
    x = x_ref[...]
    h_ref[...] = _rms_norm(x, pre_g_ref[...]).astype(BF16)

    q_ref[...] = (_dot(h_ref[...], w_in_ref[:, OFF_Q:OFF_Q + ATTN_WIDTH])
                  * (1.0 / math.sqrt(HEAD_DIM))).astype(BF16)

    kv = _dot(h_ref[...], w_in_ref[:, OFF_KV:OFF_KV + 2 * KV_WIDTH])
    k = kv[:, :KV_WIDTH]
    v = kv[:, KV_WIDTH:]
    k_sw = pltpu.roll(k, HEAD_DIM, 1)
    v_sw = pltpu.roll(v, HEAD_DIM, 1)
    low = lax.broadcasted_iota(jnp.int32, (rows, LANES), 1) < HEAD_DIM
    zero = jnp.zeros_like(v)

    @pl.when(step == 0)
    def _():
        kd_ref[:, :ATTN_BLOCK, :] = jnp.zeros((N_KV_HEADS, ATTN_BLOCK, LANES), BF16)
        vp_ref[:, :ATTN_BLOCK, :] = jnp.zeros((2 * N_KV_HEADS, ATTN_BLOCK, LANES), BF16)

    kd_ref[0, ATTN_BLOCK:, :] = jnp.where(low, k, k_sw).astype(BF16)
    kd_ref[1, ATTN_BLOCK:, :] = jnp.where(low, k_sw, k).astype(BF16)
    vp_ref[0, ATTN_BLOCK:, :] = jnp.where(low, v, zero).astype(BF16)
    vp_ref[1, ATTN_BLOCK:, :] = jnp.where(low, zero, v_sw).astype(BF16)
    vp_ref[2, ATTN_BLOCK:, :] = jnp.where(low, v_sw, zero).astype(BF16)
    vp_ref[3, ATTN_BLOCK:, :] = jnp.where(low, zero, v).astype(BF16)

    u_ref[...] = _gelu(_dot(h_ref[...], w_in_ref[:, OFF_U:OFF_U + SGU_WIDTH]))
    gv = _gelu(_dot(h_ref[...], w_in_ref[:, OFF_V:OFF_V + SGU_WIDTH]))
    mu = jnp.mean(gv, axis=-1, keepdims=True)
    gc = gv - mu
    vn = gc * lax.rsqrt(jnp.mean(gc * gc, axis=-1, keepdims=True) + LN_EPS)
    vn_ref[...] = (vn * ln_g_ref[...] + ln_b_ref[...]).astype(BF16)
    ga_ref[...] = jax.nn.sigmoid(_dot(h_ref[...], w_in_ref[:, OFF_GA:OFF_GA + D_MODEL]))
    gb_ref[...] = jax.nn.sigmoid(_dot(h_ref[...], w_in_ref[:, OFF_GB:OFF_GB + D_MODEL]))

    qi = lax.broadcasted_iota(jnp.int32, (ATTN_BLOCK, ATTN_BLOCK), 0)
    kj = lax.broadcasted_iota(jnp.int32, (ATTN_BLOCK, ATTN_BLOCK), 1)
    prev_ok = kj > qi
    cur_ok = kj <= qi
    lane_lo = lax.broadcasted_iota(jnp.int32, (ATTN_BLOCK, LANES), 1) < HEAD_DIM
    for b in range(rows // ATTN_BLOCK):
        r0 = b * ATTN_BLOCK
        if b == 0:
            valid = jnp.concatenate([jnp.logical_and(prev_ok, step > 0), cur_ok], axis=1)
        else:
            valid = jnp.concatenate([prev_ok, cur_ok], axis=1)
        for g in range(N_KV_HEADS):
            k2 = kd_ref[g, r0:r0 + 2 * ATTN_BLOCK, :]
            for pr in range(Q_PER_KV // 2):
                hp = g * (Q_PER_KV // 2) + pr
                qp = q_ref[r0:r0 + ATTN_BLOCK, hp * LANES:(hp + 1) * LANES]
                acc = None
                for par in range(2):
                    head = 2 * hp + par
                    qm = jnp.where(lane_lo if par == 0 else jnp.logical_not(lane_lo), qp,
                                   jnp.zeros_like(qp))
                    s = _dot_nt(qm, k2)
                    s = jnp.where(valid, s, MASK_VALUE)
                    sink = sinks_ref[layer, head]
                    m = jnp.maximum(jnp.max(s, axis=-1, keepdims=True), sink)
                    p = jnp.exp(s - m)
                    den = jnp.sum(p, axis=-1, keepdims=True) + jnp.exp(sink - m)
                    o = _dot(p.astype(BF16), vp_ref[2 * g + par, r0:r0 + 2 * ATTN_BLOCK, :]) / den
                    acc = o if acc is None else acc + o
                ya_ref[r0:r0 + ATTN_BLOCK, hp * LANES:(hp + 1) * LANES] = acc.astype(BF16)

    ti = lax.broadcasted_iota(jnp.int32, (SGU_CHUNK, SGU_CHUNK), 0)
    si = lax.broadcasted_iota(jnp.int32, (SGU_CHUNK, SGU_CHUNK), 1)
    causal = si <= ti
    for g in range(SGU_GROUPS):
        wg = jnp.where(causal, sgu_w_ref[g], 0.0).astype(BF16)
        bias = jnp.broadcast_to(sgu_bt_ref[:, g:g + 1], (SGU_CHUNK, SGU_GROUP_CH))
        c0 = g * SGU_GROUP_CH
        for c in range(rows // SGU_CHUNK):
            r0 = c * SGU_CHUNK
            sg = _dot(wg, vn_ref[r0:r0 + SGU_CHUNK, c0:c0 + SGU_GROUP_CH]) + bias
            ys_ref[r0:r0 + SGU_CHUNK, c0:c0 + SGU_GROUP_CH] = (
                u_ref[r0:r0 + SGU_CHUNK, c0:c0 + SGU_GROUP_CH] * sg).astype(BF16)

    merged = (ga_ref[...] * _dot(ya_ref[...], w_a_ref[...])
              + gb_ref[...] * _dot(ys_ref[...], w_s_ref[...]))
    y = _dot(merged.astype(BF16), w_o_ref[...])
    o_ref[...] = x + _rms_norm(y, post_g_ref[...])

    kd_ref[:, :ATTN_BLOCK, :] = kd_ref[:, rows:rows + ATTN_BLOCK, :]
    vp_ref[:, :ATTN_BLOCK, :] = vp_ref[:, rows:rows + ATTN_BLOCK, :]


def _mixer(x, sinks, pre_g, w_in, ln_g, ln_b, sgu_w, sgu_bt, w_a, w_s, w_o, post_g, layer):
    seq = x.shape[0]
    rows = MIX_ROWS
    row = lambda i: (i, 0)
    lay2 = lambda i: (layer, 0, 0)
    lay3 = lambda i: (layer, 0, 0, 0)
    return pl.pallas_call(
        functools.partial(_mixer_kernel, layer=layer),
        grid=(seq // rows,),
        in_specs=[
            pl.BlockSpec(memory_space=pltpu.SMEM),
            pl.BlockSpec((rows, D_MODEL), row),
            _resident((None, 1, D_MODEL), lay2),
            _resident((None, D_MODEL, IN_WIDTH), lay2),
            _resident((None, 1, SGU_WIDTH), lay2),
            _resident((None, 1, SGU_WIDTH), lay2),
            _resident((None, SGU_GROUPS, SGU_CHUNK, SGU_CHUNK), lay3),
            _resident((None, SGU_CHUNK, SGU_GROUPS), lay2),
            _resident((None, ATTN_WIDTH, D_MODEL), lay2),
            _resident((None, SGU_WIDTH, D_MODEL), lay2),
            _resident((None, D_MODEL, D_MODEL), lay2),
            _resident((None, 1, D_MODEL), lay2),
        ],
        out_specs=pl.BlockSpec((rows, D_MODEL), row),
        out_shape=jax.ShapeDtypeStruct(x.shape, x.dtype),
        scratch_shapes=[
            pltpu.VMEM((rows, D_MODEL), BF16),
            pltpu.VMEM((rows, ATTN_WIDTH), BF16),
            pltpu.VMEM((N_KV_HEADS, rows + ATTN_BLOCK, LANES), BF16),
            pltpu.VMEM((2 * N_KV_HEADS, rows + ATTN_BLOCK, LANES), BF16),
            pltpu.VMEM((rows, SGU_WIDTH), F32),
            pltpu.VMEM((rows, SGU_WIDTH), BF16),
            pltpu.VMEM((rows, D_MODEL), F32),
            pltpu.VMEM((rows, D_MODEL), F32),
            pltpu.VMEM((rows, ATTN_WIDTH), BF16),
            pltpu.VMEM((rows, SGU_WIDTH), BF16),
        ],
        compiler_params=pltpu.CompilerParams(
            dimension_semantics=("arbitrary",), vmem_limit_bytes=VMEM_LIMIT_BYTES),
        name="token_mixer",
    )(sinks, x, pre_g, w_in, ln_g, ln_b, sgu_w, sgu_bt, w_a, w_s, w_o, post_g)


def kernel(x, ffn1_pre_g, ffn1_w1, ffn1_w2, ffn1_post_g, mix_pre_g, w_in, attn_sinks, sgu_ln_g, sgu_ln_b,
           sgu_w, sgu_b, w_attn_branch, w_sgu_branch, w_out, mix_post_g, ffn2_pre_g, ffn2_w1, ffn2_w2,
           ffn2_post_g):
    batch, seq, d = x.shape
    depth = w_in.shape[0]
    assert batch == 1 and d == D_MODEL and seq % MIX_ROWS == 0 and seq % FFN_ROWS == 0
    bf = lambda w: w.astype(BF16)
    vec = lambda g: g.reshape(depth, 1, -1)
    f1w1, f1w2, f2w1, f2w2 = bf(ffn1_w1), bf(ffn1_w2), bf(ffn2_w1), bf(ffn2_w2)
    w_in_b, w_a, w_s, w_o = bf(w_in), bf(w_attn_branch), bf(w_sgu_branch), bf(w_out)
    sgu_bt = jnp.swapaxes(sgu_b, 1, 2)
    h = x.reshape(seq, d)
    for l in range(depth):
        h = _ffn(h, vec(ffn1_pre_g), f1w1, f1w2, vec(ffn1_post_g), l)
        h = _mixer(h, attn_sinks, vec(mix_pre_g), w_in_b, vec(sgu_ln_g), vec(sgu_ln_b), sgu_w, sgu_bt,
                   w_a, w_s, w_o, vec(mix_post_g), l)
        h = _ffn(h, vec(ffn2_pre_g), f2w1, f2w2, vec(ffn2_post_g), l)
    return h.reshape(batch, seq, d)
```

```python
import functools
import math

import jax
import jax.numpy as jnp
from jax import lax
from jax.experimental import pallas as pl
from jax.experimental.pallas import tpu as pltpu

D_MODEL = 1024
D_FF = 2816
HEAD_DIM = 64
N_Q_HEADS = 16
N_KV_HEADS = 2
Q_PER_KV = N_Q_HEADS // N_KV_HEADS
ATTN_WIDTH = N_Q_HEADS * HEAD_DIM
KV_WIDTH = N_KV_HEADS * HEAD_DIM
ATTN_BLOCK = 128
SGU_CHUNK = 128
SGU_GROUPS = 8
SGU_GROUP_CH = 128
SGU_WIDTH = SGU_GROUPS * SGU_GROUP_CH
OFF_Q = 0
OFF_KV = ATTN_WIDTH
OFF_U = OFF_KV + 2 * KV_WIDTH
OFF_V = OFF_U + SGU_WIDTH
OFF_GA = OFF_V + SGU_WIDTH
OFF_GB = OFF_GA + D_MODEL
IN_WIDTH = OFF_GB + D_MODEL

RMS_EPS = 1e-6
LN_EPS = 1e-5
MASK_VALUE = -1e30

LANES = 128
MXU_COLS = 256
VMEM_LIMIT_BYTES = 58 * 1024 * 1024

FFN_ROWS = 1024
FFN_SUB = 256
FFN_CHUNK = MXU_COLS
MIX_ROWS = 512
MIX_SUB = 2 * ATTN_BLOCK
MIX_CHUNK = MXU_COLS
ATTN_SKEW = 3

F32 = jnp.float32
BF16 = jnp.bfloat16


def _rms_norm(x, g):
    return x * lax.rsqrt(jnp.mean(x * x, axis=-1, keepdims=True) + RMS_EPS) * g


def _gelu(x):
    return 0.5 * x * (1.0 + lax.erf(x * math.sqrt(0.5)))


def _dot(a, b):
    return jnp.dot(a, b, preferred_element_type=F32)


def _dot_nt(a, b):
    return lax.dot_general(a, b, (((1,), (1,)), ((), ())), preferred_element_type=F32)


def _ffn_kernel(x_ref, pre_g_ref, w1_ref, w2_ref, post_g_ref, o_ref, h_ref, act_ref, y_ref):
    n_sub = FFN_ROWS // FFN_SUB

    def rows(j):
        return slice(j * FFN_SUB, (j + 1) * FFN_SUB)

    def norm(j):
        h_ref[rows(j), :] = _rms_norm(x_ref[rows(j), :], pre_g_ref[...]).astype(BF16)

    def finish(j):
        o_ref[rows(j), :] = x_ref[rows(j), :] + 0.5 * _rms_norm(y_ref[rows(j), :], post_g_ref[...])

    def down(j):
        y_ref[rows(j), :] = _dot(act_ref[rows(j), :], w2_ref[...])

    norm(0)
    for j in range(n_sub):
        for c in range(D_FF // FFN_CHUNK):
            lo = c * FFN_CHUNK
            g = _dot(h_ref[rows(j), :], w1_ref[:, lo:lo + FFN_CHUNK])
            u = _dot(h_ref[rows(j), :], w1_ref[:, D_FF + lo:D_FF + lo + FFN_CHUNK])
            act_ref[rows(j), lo:lo + FFN_CHUNK] = (jax.nn.silu(g) * u).astype(BF16)
            if c == 0 and j + 1 < n_sub:
                norm(j + 1)
            if c == 1 and j > 0:
                down(j - 1)
            if c == 3 and j > 0:
                finish(j - 1)
    down(n_sub - 1)
    finish(n_sub - 1)


def _resident(block_shape, index_map):
    return pl.BlockSpec(block_shape, index_map, pipeline_mode=pl.Buffered(1))


def _ffn(x, pre_g, w1, w2, post_g, layer):
    seq = x.shape[0]
    row = lambda i: (i, 0)
    lay2 = lambda i: (layer, 0, 0)
    return pl.pallas_call(
        _ffn_kernel,
        grid=(seq // FFN_ROWS,),
        in_specs=[
            pl.BlockSpec((FFN_ROWS, D_MODEL), row),
            _resident((None, 1, D_MODEL), lay2),
            _resident((None, D_MODEL, 2 * D_FF), lay2),
            _resident((None, D_FF, D_MODEL), lay2),
            _resident((None, 1, D_MODEL), lay2),
        ],
        out_specs=pl.BlockSpec((FFN_ROWS, D_MODEL), row),
        out_shape=jax.ShapeDtypeStruct(x.shape, x.dtype),
        scratch_shapes=[
            pltpu.VMEM((FFN_ROWS, D_MODEL), BF16),
            pltpu.VMEM((FFN_ROWS, D_FF), BF16),
            pltpu.VMEM((FFN_ROWS, D_MODEL), F32),
        ],
        compiler_params=pltpu.CompilerParams(
            dimension_semantics=("arbitrary",), vmem_limit_bytes=VMEM_LIMIT_BYTES),
        name="ffn_half_step",
    )(x, pre_g, w1, w2, post_g)


def _interleave(slots, fillers):
    n = len(slots)
    done = 0
    for t, (first, second) in enumerate(slots):
        first()
        upto = (len(fillers) * (t + 1)) // n
        for f in fillers[done:upto]:
            f()
        done = upto
        second()


def _mixer_kernel(sinks_ref, x_ref, pre_g_ref, w_in_ref, ln_g_ref, ln_b_ref, sgu_w_ref, sgu_bt_ref,
                  w_a_ref, w_s_ref, w_o_ref, post_g_ref, o_ref,
                  h_ref, q_ref, kd_ref, vp_ref, u_ref, gv_ref, vn_ref, ga_ref, gb_ref, ya_ref, ys_ref,
                  a_ref, mg_ref, y_ref, wg_ref, bias_ref, *, layer):
    step = pl.program_id(0)
    n_sub = MIX_ROWS // MIX_SUB
    n_chunk = D_MODEL // MIX_CHUNK

    @pl.when(step == 0)
    def _():
        kd_ref[:, :ATTN_BLOCK, :] = jnp.zeros((N_KV_HEADS, ATTN_BLOCK, LANES), BF16)
        vp_ref[:, :ATTN_BLOCK, :] = jnp.zeros((2 * N_KV_HEADS, ATTN_BLOCK, LANES), BF16)
        ti = lax.broadcasted_iota(jnp.int32, (SGU_CHUNK, SGU_CHUNK), 0)
        si = lax.broadcasted_iota(jnp.int32, (SGU_CHUNK, SGU_CHUNK), 1)
        for g in range(SGU_GROUPS):
            wg_ref[g] = jnp.where(si <= ti, sgu_w_ref[g], 0.0).astype(BF16)
            bias_ref[g] = jnp.broadcast_to(sgu_bt_ref[:, g:g + 1], (SGU_CHUNK, SGU_GROUP_CH))

    qi = lax.broadcasted_iota(jnp.int32, (ATTN_BLOCK, ATTN_BLOCK), 0)
    kj = lax.broadcasted_iota(jnp.int32, (ATTN_BLOCK, ATTN_BLOCK), 1)
    cur_ok = kj <= qi
    lane_lo = kj < HEAD_DIM
    low = lax.broadcasted_iota(jnp.int32, (MIX_SUB, LANES), 1) < HEAD_DIM

    def proj_items(j):
        r0 = j * MIX_SUB
        rs = slice(r0, r0 + MIX_SUB)
        ks = slice(ATTN_BLOCK + r0, ATTN_BLOCK + r0 + MIX_SUB)
        items = []

        def norm():
            h_ref[rs, :] = _rms_norm(x_ref[rs, :], pre_g_ref[...]).astype(BF16)
        items.append(norm)

        def col(off, c):
            return _dot(h_ref[rs, :], w_in_ref[:, off + c * MIX_CHUNK:off + (c + 1) * MIX_CHUNK])

        def q_item(c):
            def f():
                q_ref[rs, c * MIX_CHUNK:(c + 1) * MIX_CHUNK] = (
                    col(OFF_Q, c) * (1.0 / math.sqrt(HEAD_DIM))).astype(BF16)
            return f
        items += [q_item(c) for c in range(ATTN_WIDTH // MIX_CHUNK)]

        def kv_item():
            kv = col(OFF_KV, 0)
            k = kv[:, :KV_WIDTH]
            v = kv[:, KV_WIDTH:]
            k_sw = pltpu.roll(k, HEAD_DIM, 1)
            v_sw = pltpu.roll(v, HEAD_DIM, 1)
            zero = jnp.zeros_like(v)
            kd_ref[0, ks, :] = jnp.where(low, k, k_sw).astype(BF16)
            kd_ref[1, ks, :] = jnp.where(low, k_sw, k).astype(BF16)
            vp_ref[0, ks, :] = jnp.where(low, v, zero).astype(BF16)
            vp_ref[1, ks, :] = jnp.where(low, zero, v_sw).astype(BF16)
            vp_ref[2, ks, :] = jnp.where(low, v_sw, zero).astype(BF16)
            vp_ref[3, ks, :] = jnp.where(low, zero, v).astype(BF16)
        items.append(kv_item)

        def act_item(dst_ref, dst_rows, off, c, fn):
            def f():
                dst_ref[dst_rows, c * MIX_CHUNK:(c + 1) * MIX_CHUNK] = fn(col(off, c))
            return f
        items += [act_item(u_ref, rs, OFF_U, c, _gelu) for c in range(n_chunk)]
        items += [act_item(gv_ref, slice(0, MIX_SUB), OFF_V, c, _gelu) for c in range(n_chunk)]

        def layer_norm():
            gv = gv_ref[...]
            gc = gv - jnp.mean(gv, axis=-1, keepdims=True)
            vn = gc * lax.rsqrt(jnp.mean(gc * gc, axis=-1, keepdims=True) + LN_EPS)
            vn_ref[rs, :] = (vn * ln_g_ref[...] + ln_b_ref[...]).astype(BF16)
        items.append(layer_norm)
        items += [act_item(ga_ref, rs, OFF_GA, c, jax.nn.sigmoid) for c in range(n_chunk)]
        items += [act_item(gb_ref, rs, OFF_GB, c, jax.nn.sigmoid) for c in range(n_chunk)]
        return items

    def attn_slots(j):
        state = {}

        def scores(bi, head):
            def f():
                r0 = bi * ATTN_BLOCK
                g, hp, par = head // Q_PER_KV, head // 2, head % 2
                qp = q_ref[r0:r0 + ATTN_BLOCK, hp * LANES:(hp + 1) * LANES]
                qm = jnp.where(lane_lo if par == 0 else jnp.logical_not(lane_lo), qp, jnp.zeros_like(qp))
                s = _dot_nt(qm, kd_ref[g, r0:r0 + 2 * ATTN_BLOCK, :])
                s_prev = s[:, :ATTN_BLOCK]
                if bi == 0:
                    s_prev = jnp.where(step > 0, s_prev, MASK_VALUE)
                sm = jnp.where(cur_ok, s[:, ATTN_BLOCK:], s_prev)
                sink = sinks_ref[layer, head]
                m = jnp.maximum(jnp.max(sm, axis=-1, keepdims=True), sink)
                p = jnp.exp(sm - m)
                den = jnp.sum(p, axis=-1, keepdims=True) + jnp.exp(sink - m)
                pb = p.astype(BF16)
                zb = jnp.zeros_like(pb)
                state[(bi, head)] = (
                    jnp.concatenate([jnp.where(cur_ok, zb, pb), jnp.where(cur_ok, pb, zb)], axis=1), den)
            return f

        def weighted_sum(bi, head):
            def f():
                r0 = bi * ATTN_BLOCK
                g, hp, par = head // Q_PER_KV, head // 2, head % 2
                p2, den = state.pop((bi, head))
                o = _dot(p2, vp_ref[2 * g + par, r0:r0 + 2 * ATTN_BLOCK, :]) / den
                if par == 0:
                    state["acc"] = o
                else:
                    ya_ref[r0:r0 + ATTN_BLOCK, hp * LANES:(hp + 1) * LANES] = (
                        state.pop("acc") + o).astype(BF16)
            return f

        work = [(2 * j + b, head) for b in range(MIX_SUB // ATTN_BLOCK) for head in range(N_Q_HEADS)]
        nothing = lambda: None
        slots = []
        for t in range(len(work) + ATTN_SKEW):
            first = scores(*work[t]) if t < len(work) else nothing
            second = weighted_sum(*work[t - ATTN_SKEW]) if t >= ATTN_SKEW else nothing
            slots.append((first, second))
        return slots

    def out_items(j):
        r0 = j * MIX_SUB
        rs = slice(r0, r0 + MIX_SUB)
        ra = slice(r0, r0 + SGU_CHUNK)
        rb = slice(r0 + SGU_CHUNK, r0 + 2 * SGU_CHUNK)
        items = []

        def sgu_item(g):
            def f():
                cg = slice(g * SGU_GROUP_CH, (g + 1) * SGU_GROUP_CH)
                rhs = jnp.concatenate([vn_ref[ra, cg], vn_ref[rb, cg]], axis=1)
                sg = _dot(wg_ref[g], rhs)
                bias = bias_ref[g]
                ys_ref[ra, cg] = (u_ref[ra, cg] * (sg[:, :SGU_GROUP_CH] + bias)).astype(BF16)
                ys_ref[rb, cg] = (u_ref[rb, cg] * (sg[:, SGU_GROUP_CH:] + bias)).astype(BF16)
            return f

        def attn_branch_item(c):
            def f():
                cs = slice(c * MIX_CHUNK, (c + 1) * MIX_CHUNK)
                a_ref[:, cs] = ga_ref[rs, cs] * _dot(ya_ref[rs, :], w_a_ref[:, cs])
            return f

        sgu_per_dot = SGU_GROUPS // n_chunk
        for c in range(n_chunk):
            items.append(attn_branch_item(c))
            items += [sgu_item(g) for g in range(c * sgu_per_dot, (c + 1) * sgu_per_dot)]

        def merge_item(c):
            def f():
                cs = slice(c * MIX_CHUNK, (c + 1) * MIX_CHUNK)
                s = _dot(ys_ref[rs, :], w_s_ref[:, cs])
                mg_ref[:, cs] = (a_ref[:, cs] + gb_ref[rs, cs] * s).astype(BF16)
            return f
        items += [merge_item(c) for c in range(n_chunk)]

        def proj_item(c):
            def f():
                cs = slice(c * MIX_CHUNK, (c + 1) * MIX_CHUNK)
                y_ref[:, cs] = _dot(mg_ref[...], w_o_ref[:, cs])
            return f
        items += [proj_item(c) for c in range(n_chunk)]

        def finish():
            o_ref[rs, :] = x_ref[rs, :] + _rms_norm(y_ref[...], post_g_ref[...])
        items.append(finish)
        return items

    for stage in range(n_sub + 2):
        fillers = []
        if stage < n_sub:
            fillers += proj_items(stage)
        if stage >= 2:
            fillers += out_items(stage - 2)
        if 1 <= stage <= n_sub:
            _interleave(attn_slots(stage - 1), fillers)
        else:
            for f in fillers:
                f()

    kd_ref[:, :ATTN_BLOCK, :] = kd_ref[:, MIX_ROWS:MIX_ROWS + ATTN_BLOCK, :]
    vp_ref[:, :ATTN_BLOCK, :] = vp_ref[:, MIX_ROWS:MIX_ROWS + ATTN_BLOCK, :]


def _mixer(x, sinks, pre_g, w_in, ln_g, ln_b, sgu_w, sgu_bt, w_a, w_s, w_o, post_g, layer):
    seq = x.shape[0]
    rows = MIX_ROWS
    row = lambda i: (i, 0)
    lay2 = lambda i: (layer, 0, 0)
    lay3 = lambda i: (layer, 0, 0, 0)
    return pl.pallas_call(
        functools.partial(_mixer_kernel, layer=layer),
        grid=(seq // rows,),
        in_specs=[
            pl.BlockSpec(memory_space=pltpu.SMEM),
            pl.BlockSpec((rows, D_MODEL), row),
            _resident((None, 1, D_MODEL), lay2),
            _resident((None, D_MODEL, IN_WIDTH), lay2),
            _resident((None, 1, SGU_WIDTH), lay2),
            _resident((None, 1, SGU_WIDTH), lay2),
            _resident((None, SGU_GROUPS, SGU_CHUNK, SGU_CHUNK), lay3),
            _resident((None, SGU_CHUNK, SGU_GROUPS), lay2),
            _resident((None, ATTN_WIDTH, D_MODEL), lay2),
            _resident((None, SGU_WIDTH, D_MODEL), lay2),
            _resident((None, D_MODEL, D_MODEL), lay2),
            _resident((None, 1, D_MODEL), lay2),
        ],
        out_specs=pl.BlockSpec((rows, D_MODEL), row),
        out_shape=jax.ShapeDtypeStruct(x.shape, x.dtype),
        scratch_shapes=[
            pltpu.VMEM((rows, D_MODEL), BF16),
            pltpu.VMEM((rows, ATTN_WIDTH), BF16),
            pltpu.VMEM((N_KV_HEADS, rows + ATTN_BLOCK, LANES), BF16),
            pltpu.VMEM((2 * N_KV_HEADS, rows + ATTN_BLOCK, LANES), BF16),
            pltpu.VMEM((rows, SGU_WIDTH), F32),
            pltpu.VMEM((MIX_SUB, SGU_WIDTH), F32),
            pltpu.VMEM((rows, SGU_WIDTH), BF16),
            pltpu.VMEM((rows, D_MODEL), F32),
            pltpu.VMEM((rows, D_MODEL), F32),
            pltpu.VMEM((rows, ATTN_WIDTH), BF16),
            pltpu.VMEM((rows, SGU_WIDTH), BF16),
            pltpu.VMEM((MIX_SUB, D_MODEL), F32),
            pltpu.VMEM((MIX_SUB, D_MODEL), BF16),
            pltpu.VMEM((MIX_SUB, D_MODEL), F32),
            pltpu.VMEM((SGU_GROUPS, SGU_CHUNK, SGU_CHUNK), BF16),
            pltpu.VMEM((SGU_GROUPS, SGU_CHUNK, SGU_GROUP_CH), F32),
        ],
        compiler_params=pltpu.CompilerParams(
            dimension_semantics=("arbitrary",), vmem_limit_bytes=VMEM_LIMIT_BYTES),
        name="token_mixer",
    )(sinks, x, pre_g, w_in, ln_g, ln_b, sgu_w, sgu_bt, w_a, w_s, w_o, post_g)


def kernel(x, ffn1_pre_g, ffn1_w1, ffn1_w2, ffn1_post_g, mix_pre_g, w_in, attn_sinks, sgu_ln_g, sgu_ln_b,
           sgu_w, sgu_b, w_attn_branch, w_sgu_branch, w_out, mix_post_g, ffn2_pre_g, ffn2_w1, ffn2_w2,
           ffn2_post_g):
    batch, seq, d = x.shape
    depth = w_in.shape[0]
    assert batch == 1 and d == D_MODEL and seq % MIX_ROWS == 0 and seq % FFN_ROWS == 0
    bf = lambda w: w.astype(BF16)
    vec = lambda g: g.reshape(depth, 1, -1)
    f1w1, f1w2, f2w1, f2w2 = bf(ffn1_w1), bf(ffn1_w2), bf(ffn2_w1), bf(ffn2_w2)
    w_in_b, w_a, w_s, w_o = bf(w_in), bf(w_attn_branch), bf(w_sgu_branch), bf(w_out)
    sgu_bt = jnp.swapaxes(sgu_b, 1, 2)
    h = x.reshape(seq, d)
    for l in range(depth):
        h = _ffn(h, vec(ffn1_pre_g), f1w1, f1w2, vec(ffn1_post_g), l)
        h = _mixer(h, attn_sinks, vec(mix_pre_g), w_in_b, vec(sgu_ln_g), vec(sgu_ln_b), sgu_w, sgu_bt,
                   w_a, w_s, w_o, vec(mix_post_g), l)
        h = _ffn(h, vec(ffn2_pre_g), f2w1, f2w2, vec(ffn2_post_g), l)
    return h.reshape(batch, seq, d)
```

```python
import functools
import math

import jax
import jax.numpy as jnp
from jax import lax
from jax.experimental import pallas as pl
from jax.experimental.pallas import tpu as pltpu

D_MODEL = 1024
D_FF = 2816
HEAD_DIM = 64
N_Q_HEADS = 16
N_KV_HEADS = 2
Q_PER_KV = N_Q_HEADS // N_KV_HEADS
ATTN_WIDTH = N_Q_HEADS * HEAD_DIM
KV_WIDTH = N_KV_HEADS * HEAD_DIM
ATTN_BLOCK = 128
SGU_CHUNK = 128
SGU_GROUPS = 8
SGU_GROUP_CH = 128
SGU_WIDTH = SGU_GROUPS * SGU_GROUP_CH
OFF_Q = 0
OFF_KV = ATTN_WIDTH
OFF_U = OFF_KV + 2 * KV_WIDTH
OFF_V = OFF_U + SGU_WIDTH
OFF_GA = OFF_V + SGU_WIDTH
OFF_GB = OFF_GA + D_MODEL
IN_WIDTH = OFF_GB + D_MODEL

RMS_EPS = 1e-6
LN_EPS = 1e-5
MASK_VALUE = -1e30

LANES = 128
BF16_SUBLANES = 16
MXU_COLS = 256
VMEM_LIMIT_BYTES = 58 * 1024 * 1024

FFN_ROWS = 1024
FFN_SUB = 256
FFN_CHUNK = MXU_COLS
MIX_ROWS = 512
MIX_SUB = 2 * ATTN_BLOCK
MIX_CHUNK = MXU_COLS
ATTN_SKEW = 3

F32 = jnp.float32
BF16 = jnp.bfloat16


def _rms_norm(x, g):
    return x * lax.rsqrt(jnp.mean(x * x, axis=-1, keepdims=True) + RMS_EPS) * g


def _gelu(x):
    return 0.5 * x * (1.0 + lax.erf(x * math.sqrt(0.5)))


def _dot(a, b):
    return jnp.dot(a, b, preferred_element_type=F32)


def _dot_nt(a, b):
    return lax.dot_general(a, b, (((1,), (1,)), ((), ())), preferred_element_type=F32)


def _cast_slabs(src_refs, dst_refs):
    for src, dst in zip(src_refs, dst_refs):
        dst[...] = src[...].astype(BF16)


def _ffn_kernel(x_ref, pre_g_ref, w1_ref, w2_ref, post_g_ref, *refs, n_cast):
    cast_src, o_ref, cast_dst = refs[:n_cast], refs[n_cast], refs[n_cast + 1:2 * n_cast + 1]
    h_ref, act_ref, y_ref = refs[2 * n_cast + 1:]
    _cast_slabs(cast_src, cast_dst)
    n_sub = FFN_ROWS // FFN_SUB

    def rows(j):
        return slice(j * FFN_SUB, (j + 1) * FFN_SUB)

    def norm(j):
        h_ref[rows(j), :] = _rms_norm(x_ref[rows(j), :], pre_g_ref[...]).astype(BF16)

    def finish(j):
        o_ref[rows(j), :] = x_ref[rows(j), :] + 0.5 * _rms_norm(y_ref[rows(j), :], post_g_ref[...])

    def down(j):
        y_ref[rows(j), :] = _dot(act_ref[rows(j), :], w2_ref[...])

    norm(0)
    for j in range(n_sub):
        for c in range(D_FF // FFN_CHUNK):
            lo = c * FFN_CHUNK
            g = _dot(h_ref[rows(j), :], w1_ref[:, lo:lo + FFN_CHUNK])
            u = _dot(h_ref[rows(j), :], w1_ref[:, D_FF + lo:D_FF + lo + FFN_CHUNK])
            act_ref[rows(j), lo:lo + FFN_CHUNK] = (jax.nn.silu(g) * u).astype(BF16)
            if c == 0 and j + 1 < n_sub:
                norm(j + 1)
            if c == 1 and j > 0:
                down(j - 1)
            if c == 3 and j > 0:
                finish(j - 1)
    down(n_sub - 1)
    finish(n_sub - 1)


def _resident(block_shape, index_map):
    return pl.BlockSpec(block_shape, index_map, pipeline_mode=pl.Buffered(1))


def _cast_specs(cast, steps):
    in_specs, out_specs, out_shapes = [], [], []
    for w, layer in cast:
        _, r, c = w.shape
        slab = r // steps
        assert slab * steps == r and slab % BF16_SUBLANES == 0
        in_specs.append(pl.BlockSpec((None, slab, c), lambda i, layer=layer: (layer, i, 0)))
        out_specs.append(pl.BlockSpec((slab, c), lambda i: (i, 0)))
        out_shapes.append(jax.ShapeDtypeStruct((r, c), BF16))
    return in_specs, out_specs, out_shapes


def _ffn(x, pre_g, w1, w2, post_g, layer, cast):
    seq = x.shape[0]
    steps = seq // FFN_ROWS
    row = lambda i: (i, 0)
    lay2 = lambda i: (layer, 0, 0)
    whole = lambda i: (0, 0)
    cast_in, cast_out, cast_shapes = _cast_specs(cast, steps)
    out = pl.pallas_call(
        functools.partial(_ffn_kernel, n_cast=len(cast)),
        grid=(steps,),
        in_specs=[
            pl.BlockSpec((FFN_ROWS, D_MODEL), row),
            _resident((None, 1, D_MODEL), lay2),
            _resident((D_MODEL, 2 * D_FF), whole),
            _resident((D_FF, D_MODEL), whole),
            _resident((None, 1, D_MODEL), lay2),
        ] + cast_in,
        out_specs=[pl.BlockSpec((FFN_ROWS, D_MODEL), row)] + cast_out,
        out_shape=[jax.ShapeDtypeStruct(x.shape, x.dtype)] + cast_shapes,
        scratch_shapes=[
            pltpu.VMEM((FFN_ROWS, D_MODEL), BF16),
            pltpu.VMEM((FFN_ROWS, D_FF), BF16),
            pltpu.VMEM((FFN_ROWS, D_MODEL), F32),
        ],
        compiler_params=pltpu.CompilerParams(
            dimension_semantics=("arbitrary",), vmem_limit_bytes=VMEM_LIMIT_BYTES),
        name="ffn_half_step",
    )(x, pre_g, w1, w2, post_g, *[w for w, _ in cast])
    return out[0], out[1:]


def _interleave(slots, fillers):
    n = len(slots)
    done = 0
    for t, (first, second) in enumerate(slots):
        first()
        upto = (len(fillers) * (t + 1)) // n
        for f in fillers[done:upto]:
            f()
        done = upto
        second()


def _mixer_kernel(sinks_ref, x_ref, pre_g_ref, w_in_ref, ln_g_ref, ln_b_ref, sgu_w_ref, sgu_bt_ref,
                  w_a_ref, w_s_ref, w_o_ref, post_g_ref, *refs, layer, n_cast):
    cast_src, o_ref, cast_dst = refs[:n_cast], refs[n_cast], refs[n_cast + 1:2 * n_cast + 1]
    (h_ref, q_ref, kd_ref, vp_ref, u_ref, gv_ref, vn_ref, ga_ref, gb_ref, ya_ref, ys_ref,
     a_ref, mg_ref, y_ref, wg_ref, bias_ref) = refs[2 * n_cast + 1:]
    _cast_slabs(cast_src, cast_dst)
    step = pl.program_id(0)
    n_sub = MIX_ROWS // MIX_SUB
    n_chunk = D_MODEL // MIX_CHUNK

    @pl.when(step == 0)
    def _():
        kd_ref[:, :ATTN_BLOCK, :] = jnp.zeros((N_KV_HEADS, ATTN_BLOCK, LANES), BF16)
        vp_ref[:, :ATTN_BLOCK, :] = jnp.zeros((2 * N_KV_HEADS, ATTN_BLOCK, LANES), BF16)
        ti = lax.broadcasted_iota(jnp.int32, (SGU_CHUNK, SGU_CHUNK), 0)
        si = lax.broadcasted_iota(jnp.int32, (SGU_CHUNK, SGU_CHUNK), 1)
        for g in range(SGU_GROUPS):
            wg_ref[g] = jnp.where(si <= ti, sgu_w_ref[g], 0.0).astype(BF16)
            bias_ref[g] = jnp.broadcast_to(sgu_bt_ref[:, g:g + 1], (SGU_CHUNK, SGU_GROUP_CH))

    qi = lax.broadcasted_iota(jnp.int32, (ATTN_BLOCK, ATTN_BLOCK), 0)
    kj = lax.broadcasted_iota(jnp.int32, (ATTN_BLOCK, ATTN_BLOCK), 1)
    cur_ok = kj <= qi
    lane_lo = kj < HEAD_DIM
    low = lax.broadcasted_iota(jnp.int32, (MIX_SUB, LANES), 1) < HEAD_DIM

    def proj_items(j):
        r0 = j * MIX_SUB
        rs = slice(r0, r0 + MIX_SUB)
        ks = slice(ATTN_BLOCK + r0, ATTN_BLOCK + r0 + MIX_SUB)
        items = []

        def norm():
            h_ref[rs, :] = _rms_norm(x_ref[rs, :], pre_g_ref[...]).astype(BF16)
        items.append(norm)

        def col(off, c):
            return _dot(h_ref[rs, :], w_in_ref[:, off + c * MIX_CHUNK:off + (c + 1) * MIX_CHUNK])

        def q_item(c):
            def f():
                q_ref[rs, c * MIX_CHUNK:(c + 1) * MIX_CHUNK] = (
                    col(OFF_Q, c) * (1.0 / math.sqrt(HEAD_DIM))).astype(BF16)
            return f
        items += [q_item(c) for c in range(ATTN_WIDTH // MIX_CHUNK)]

        def kv_item():
            kv = col(OFF_KV, 0)
            k = kv[:, :KV_WIDTH]
            v = kv[:, KV_WIDTH:]
            k_sw = pltpu.roll(k, HEAD_DIM, 1)
            v_sw = pltpu.roll(v, HEAD_DIM, 1)
            zero = jnp.zeros_like(v)
            kd_ref[0, ks, :] = jnp.where(low, k, k_sw).astype(BF16)
            kd_ref[1, ks, :] = jnp.where(low, k_sw, k).astype(BF16)
            vp_ref[0, ks, :] = jnp.where(low, v, zero).astype(BF16)
            vp_ref[1, ks, :] = jnp.where(low, zero, v_sw).astype(BF16)
            vp_ref[2, ks, :] = jnp.where(low, v_sw, zero).astype(BF16)
            vp_ref[3, ks, :] = jnp.where(low, zero, v).astype(BF16)
        items.append(kv_item)

        def raw_item(dst_ref, dst_rows, off, c):
            def f():
                dst_ref[dst_rows, c * MIX_CHUNK:(c + 1) * MIX_CHUNK] = col(off, c)
            return f
        items += [raw_item(u_ref, rs, OFF_U, c) for c in range(n_chunk)]
        items += [raw_item(gv_ref, slice(0, MIX_SUB), OFF_V, c) for c in range(n_chunk)]

        def layer_norm():
            gv = _gelu(gv_ref[...])
            gc = gv - jnp.mean(gv, axis=-1, keepdims=True)
            vn = gc * lax.rsqrt(jnp.mean(gc * gc, axis=-1, keepdims=True) + LN_EPS)
            vn_ref[rs, :] = (vn * ln_g_ref[...] + ln_b_ref[...]).astype(BF16)
        items.append(layer_norm)
        items += [raw_item(ga_ref, rs, OFF_GA, c) for c in range(n_chunk)]
        items += [raw_item(gb_ref, rs, OFF_GB, c) for c in range(n_chunk)]
        return items

    def attn_slots(j):
        state = {}

        def scores(bi, head):
            def f():
                r0 = bi * ATTN_BLOCK
                g, hp, par = head // Q_PER_KV, head // 2, head % 2
                qp = q_ref[r0:r0 + ATTN_BLOCK, hp * LANES:(hp + 1) * LANES]
                qm = jnp.where(lane_lo if par == 0 else jnp.logical_not(lane_lo), qp, jnp.zeros_like(qp))
                s = _dot_nt(qm, kd_ref[g, r0:r0 + 2 * ATTN_BLOCK, :])
                s_prev = s[:, :ATTN_BLOCK]
                if bi == 0:
                    s_prev = jnp.where(step > 0, s_prev, MASK_VALUE)
                sm = jnp.where(cur_ok, s[:, ATTN_BLOCK:], s_prev)
                sink = sinks_ref[layer, head]
                m = jnp.maximum(jnp.max(sm, axis=-1, keepdims=True), sink)
                p = jnp.exp(sm - m)
                den = jnp.sum(p, axis=-1, keepdims=True) + jnp.exp(sink - m)
                pb = p.astype(BF16)
                zb = jnp.zeros_like(pb)
                state[(bi, head)] = (
                    jnp.concatenate([jnp.where(cur_ok, zb, pb), jnp.where(cur_ok, pb, zb)], axis=1), den)
            return f

        def weighted_sum(bi, head):
            def f():
                r0 = bi * ATTN_BLOCK
                g, hp, par = head // Q_PER_KV, head // 2, head % 2
                p2, den = state.pop((bi, head))
                o = _dot(p2, vp_ref[2 * g + par, r0:r0 + 2 * ATTN_BLOCK, :]) / den
                if par == 0:
                    state["acc"] = o
                else:
                    ya_ref[r0:r0 + ATTN_BLOCK, hp * LANES:(hp + 1) * LANES] = (
                        state.pop("acc") + o).astype(BF16)
            return f

        work = [(2 * j + b, head) for b in range(MIX_SUB // ATTN_BLOCK) for head in range(N_Q_HEADS)]
        nothing = lambda: None
        slots = []
        for t in range(len(work) + ATTN_SKEW):
            first = scores(*work[t]) if t < len(work) else nothing
            second = weighted_sum(*work[t - ATTN_SKEW]) if t >= ATTN_SKEW else nothing
            slots.append((first, second))
        return slots

    def out_items(j):
        r0 = j * MIX_SUB
        rs = slice(r0, r0 + MIX_SUB)
        ra = slice(r0, r0 + SGU_CHUNK)
        rb = slice(r0 + SGU_CHUNK, r0 + 2 * SGU_CHUNK)
        items = []

        def sgu_item(g):
            def f():
                cg = slice(g * SGU_GROUP_CH, (g + 1) * SGU_GROUP_CH)
                rhs = jnp.concatenate([vn_ref[ra, cg], vn_ref[rb, cg]], axis=1)
                sg = _dot(wg_ref[g], rhs)
                bias = bias_ref[g]
                ys_ref[ra, cg] = (_gelu(u_ref[ra, cg]) * (sg[:, :SGU_GROUP_CH] + bias)).astype(BF16)
                ys_ref[rb, cg] = (_gelu(u_ref[rb, cg]) * (sg[:, SGU_GROUP_CH:] + bias)).astype(BF16)
            return f

        def attn_branch_item(c):
            def f():
                cs = slice(c * MIX_CHUNK, (c + 1) * MIX_CHUNK)
                a_ref[:, cs] = jax.nn.sigmoid(ga_ref[rs, cs]) * _dot(ya_ref[rs, :], w_a_ref[:, cs])
            return f

        sgu_per_dot = SGU_GROUPS // n_chunk
        for c in range(n_chunk):
            items.append(attn_branch_item(c))
            items += [sgu_item(g) for g in range(c * sgu_per_dot, (c + 1) * sgu_per_dot)]

        def merge_item(c):
            def f():
                cs = slice(c * MIX_CHUNK, (c + 1) * MIX_CHUNK)
                s = _dot(ys_ref[rs, :], w_s_ref[:, cs])
                mg_ref[:, cs] = (a_ref[:, cs] + jax.nn.sigmoid(gb_ref[rs, cs]) * s).astype(BF16)
            return f
        items += [merge_item(c) for c in range(n_chunk)]

        def proj_item(c):
            def f():
                cs = slice(c * MIX_CHUNK, (c + 1) * MIX_CHUNK)
                y_ref[:, cs] = _dot(mg_ref[...], w_o_ref[:, cs])
            return f
        items += [proj_item(c) for c in range(n_chunk)]

        def finish():
            o_ref[rs, :] = x_ref[rs, :] + _rms_norm(y_ref[...], post_g_ref[...])
        items.append(finish)
        return items

    n_late = n_chunk
    for stage in range(n_sub + 2):
        fillers = []
        if 1 <= stage <= n_sub:
            fillers += proj_items(stage - 1)[-n_late:]
        if stage < n_sub:
            fillers += proj_items(stage)[:-n_late]
        if stage >= 2:
            fillers += out_items(stage - 2)
        if 1 <= stage <= n_sub:
            _interleave(attn_slots(stage - 1), fillers)
        else:
            for f in fillers:
                f()

    kd_ref[:, :ATTN_BLOCK, :] = kd_ref[:, MIX_ROWS:MIX_ROWS + ATTN_BLOCK, :]
    vp_ref[:, :ATTN_BLOCK, :] = vp_ref[:, MIX_ROWS:MIX_ROWS + ATTN_BLOCK, :]


def _mixer(x, sinks, pre_g, w_in, ln_g, ln_b, sgu_w, sgu_bt, w_a, w_s, w_o, post_g, layer, cast):
    seq = x.shape[0]
    rows = MIX_ROWS
    steps = seq // rows
    row = lambda i: (i, 0)
    lay2 = lambda i: (layer, 0, 0)
    lay3 = lambda i: (layer, 0, 0, 0)
    whole = lambda i: (0, 0)
    cast_in, cast_out, cast_shapes = _cast_specs(cast, steps)
    out = pl.pallas_call(
        functools.partial(_mixer_kernel, layer=layer, n_cast=len(cast)),
        grid=(steps,),
        in_specs=[
            pl.BlockSpec(memory_space=pltpu.SMEM),
            pl.BlockSpec((rows, D_MODEL), row),
            _resident((None, 1, D_MODEL), lay2),
            _resident((D_MODEL, IN_WIDTH), whole),
            _resident((None, 1, SGU_WIDTH), lay2),
            _resident((None, 1, SGU_WIDTH), lay2),
            _resident((None, SGU_GROUPS, SGU_CHUNK, SGU_CHUNK), lay3),
            _resident((None, SGU_CHUNK, SGU_GROUPS), lay2),
            _resident((ATTN_WIDTH, D_MODEL), whole),
            _resident((SGU_WIDTH, D_MODEL), whole),
            _resident((D_MODEL, D_MODEL), whole),
            _resident((None, 1, D_MODEL), lay2),
        ] + cast_in,
        out_specs=[pl.BlockSpec((rows, D_MODEL), row)] + cast_out,
        out_shape=[jax.ShapeDtypeStruct(x.shape, x.dtype)] + cast_shapes,
        scratch_shapes=[
            pltpu.VMEM((rows, D_MODEL), BF16),
            pltpu.VMEM((rows, ATTN_WIDTH), BF16),
            pltpu.VMEM((N_KV_HEADS, rows + ATTN_BLOCK, LANES), BF16),
            pltpu.VMEM((2 * N_KV_HEADS, rows + ATTN_BLOCK, LANES), BF16),
            pltpu.VMEM((rows, SGU_WIDTH), F32),
            pltpu.VMEM((MIX_SUB, SGU_WIDTH), F32),
            pltpu.VMEM((rows, SGU_WIDTH), BF16),
            pltpu.VMEM((rows, D_MODEL), F32),
            pltpu.VMEM((rows, D_MODEL), F32),
            pltpu.VMEM((rows, ATTN_WIDTH), BF16),
            pltpu.VMEM((rows, SGU_WIDTH), BF16),
            pltpu.VMEM((MIX_SUB, D_MODEL), F32),
            pltpu.VMEM((MIX_SUB, D_MODEL), BF16),
            pltpu.VMEM((MIX_SUB, D_MODEL), F32),
            pltpu.VMEM((SGU_GROUPS, SGU_CHUNK, SGU_CHUNK), BF16),
            pltpu.VMEM((SGU_GROUPS, SGU_CHUNK, SGU_GROUP_CH), F32),
        ],
        compiler_params=pltpu.CompilerParams(
            dimension_semantics=("arbitrary",), vmem_limit_bytes=VMEM_LIMIT_BYTES),
        name="token_mixer",
    )(sinks, x, pre_g, w_in, ln_g, ln_b, sgu_w, sgu_bt, w_a, w_s, w_o, post_g, *[w for w, _ in cast])
    return out[0], out[1:]


def kernel(x, ffn1_pre_g, ffn1_w1, ffn1_w2, ffn1_post_g, mix_pre_g, w_in, attn_sinks, sgu_ln_g, sgu_ln_b,
           sgu_w, sgu_b, w_attn_branch, w_sgu_branch, w_out, mix_post_g, ffn2_pre_g, ffn2_w1, ffn2_w2,
           ffn2_post_g):
    batch, seq, d = x.shape
    depth = w_in.shape[0]
    assert batch == 1 and d == D_MODEL and seq % MIX_ROWS == 0 and seq % FFN_ROWS == 0
    vec = lambda g: g.reshape(depth, 1, -1)
    sgu_bt = jnp.swapaxes(sgu_b, 1, 2)
    h = x.reshape(seq, d)
    f1w1, f1w2 = ffn1_w1[0].astype(BF16), ffn1_w2[0].astype(BF16)
    for l in range(depth):
        h, (w_in_b, w_a, w_s, w_o, f2w2) = _ffn(
            h, vec(ffn1_pre_g), f1w1, f1w2, vec(ffn1_post_g), l,
            [(w_in, l), (w_attn_branch, l), (w_sgu_branch, l), (w_out, l), (ffn2_w2, l)])
        h, (f2w1,) = _mixer(h, attn_sinks, vec(mix_pre_g), w_in_b, vec(sgu_ln_g), vec(sgu_ln_b), sgu_w, sgu_bt,
                            w_a, w_s, w_o, vec(mix_post_g), l, [(ffn2_w1, l)])
        nxt = [(ffn1_w1, l + 1), (ffn1_w2, l + 1)] if l + 1 < depth else []
        h, nxt_w = _ffn(h, vec(ffn2_pre_g), f2w1, f2w2, vec(ffn2_post_g), l, nxt)
        if nxt:
            f1w1, f1w2 = nxt_w
    return h.reshape(batch, seq, d)
```

```python
import functools
import math

import jax
import jax.numpy as jnp
from jax import lax
from jax.experimental import pallas as pl
from jax.experimental.pallas import tpu as pltpu

D_MODEL = 1024
D_FF = 2816
HEAD_DIM = 64
N_Q_HEADS = 16
N_KV_HEADS = 2
Q_PER_KV = N_Q_HEADS // N_KV_HEADS
ATTN_WIDTH = N_Q_HEADS * HEAD_DIM
KV_WIDTH = N_KV_HEADS * HEAD_DIM
ATTN_BLOCK = 128
SGU_CHUNK = 128
SGU_GROUPS = 8
SGU_GROUP_CH = 128
SGU_WIDTH = SGU_GROUPS * SGU_GROUP_CH
OFF_Q = 0
OFF_KV = ATTN_WIDTH
OFF_U = OFF_KV + 2 * KV_WIDTH
OFF_V = OFF_U + SGU_WIDTH
OFF_GA = OFF_V + SGU_WIDTH
OFF_GB = OFF_GA + D_MODEL
IN_WIDTH = OFF_GB + D_MODEL

RMS_EPS = 1e-6
LN_EPS = 1e-5
MASK_VALUE = -1e30

LANES = 128
BF16_SUBLANES = 16
MXU_COLS = 256
VMEM_LIMIT_BYTES = 58 * 1024 * 1024

FFN_ROWS = 1024
FFN_SUB = 256
FFN_CHUNK = MXU_COLS
MIX_ROWS = 512
MIX_SUB = 2 * ATTN_BLOCK
MIX_CHUNK = MXU_COLS
ATTN_SKEW = 6

F32 = jnp.float32
BF16 = jnp.bfloat16


def _rms_norm(x, g):
    return x * lax.rsqrt(jnp.mean(x * x, axis=-1, keepdims=True) + RMS_EPS) * g


def _gelu(x):
    return 0.5 * x * (1.0 + lax.erf(x * math.sqrt(0.5)))


def _dot(a, b):
    return jnp.dot(a, b, preferred_element_type=F32)


def _dot_nt(a, b):
    return lax.dot_general(a, b, (((1,), (1,)), ((), ())), preferred_element_type=F32)


def _cast_slabs(src_refs, dst_refs):
    for src, dst in zip(src_refs, dst_refs):
        dst[...] = src[...].astype(BF16)


def _ffn_kernel(x_ref, pre_g_ref, w1_ref, w2_ref, post_g_ref, *refs, n_cast):
    cast_src, o_ref, cast_dst = refs[:n_cast], refs[n_cast], refs[n_cast + 1:2 * n_cast + 1]
    h_ref, act_ref, y_ref = refs[2 * n_cast + 1:]
    _cast_slabs(cast_src, cast_dst)
    n_sub = FFN_ROWS // FFN_SUB

    def rows(j):
        return slice(j * FFN_SUB, (j + 1) * FFN_SUB)

    def norm(j):
        h_ref[rows(j), :] = _rms_norm(x_ref[rows(j), :], pre_g_ref[...]).astype(BF16)

    def finish(j):
        o_ref[rows(j), :] = x_ref[rows(j), :] + 0.5 * _rms_norm(y_ref[rows(j), :], post_g_ref[...])

    def down(j):
        y_ref[rows(j), :] = _dot(act_ref[rows(j), :], w2_ref[...])

    norm(0)
    for j in range(n_sub):
        for c in range(D_FF // FFN_CHUNK):
            lo = c * FFN_CHUNK
            g = _dot(h_ref[rows(j), :], w1_ref[:, lo:lo + FFN_CHUNK])
            u = _dot(h_ref[rows(j), :], w1_ref[:, D_FF + lo:D_FF + lo + FFN_CHUNK])
            act_ref[rows(j), lo:lo + FFN_CHUNK] = (jax.nn.silu(g) * u).astype(BF16)
            if c == 0 and j + 1 < n_sub:
                norm(j + 1)
            if c == 1 and j > 0:
                down(j - 1)
            if c == 3 and j > 0:
                finish(j - 1)
    down(n_sub - 1)
    finish(n_sub - 1)


def _resident(block_shape, index_map):
    return pl.BlockSpec(block_shape, index_map, pipeline_mode=pl.Buffered(1))


def _cast_specs(cast, steps):
    in_specs, out_specs, out_shapes = [], [], []
    for w, layer in cast:
        _, r, c = w.shape
        slab = r // steps
        assert slab * steps == r and slab % BF16_SUBLANES == 0
        in_specs.append(pl.BlockSpec((None, slab, c), lambda i, layer=layer: (layer, i, 0)))
        out_specs.append(pl.BlockSpec((slab, c), lambda i: (i, 0)))
        out_shapes.append(jax.ShapeDtypeStruct((r, c), BF16))
    return in_specs, out_specs, out_shapes


def _ffn(x, pre_g, w1, w2, post_g, layer, cast):
    seq = x.shape[0]
    steps = seq // FFN_ROWS
    row = lambda i: (i, 0)
    lay2 = lambda i: (layer, 0, 0)
    whole = lambda i: (0, 0)
    cast_in, cast_out, cast_shapes = _cast_specs(cast, steps)
    out = pl.pallas_call(
        functools.partial(_ffn_kernel, n_cast=len(cast)),
        grid=(steps,),
        in_specs=[
            pl.BlockSpec((FFN_ROWS, D_MODEL), row),
            _resident((None, 1, D_MODEL), lay2),
            _resident((D_MODEL, 2 * D_FF), whole),
            _resident((D_FF, D_MODEL), whole),
            _resident((None, 1, D_MODEL), lay2),
        ] + cast_in,
        out_specs=[pl.BlockSpec((FFN_ROWS, D_MODEL), row)] + cast_out,
        out_shape=[jax.ShapeDtypeStruct(x.shape, x.dtype)] + cast_shapes,
        scratch_shapes=[
            pltpu.VMEM((FFN_ROWS, D_MODEL), BF16),
            pltpu.VMEM((FFN_ROWS, D_FF), BF16),
            pltpu.VMEM((FFN_ROWS, D_MODEL), F32),
        ],
        compiler_params=pltpu.CompilerParams(
            dimension_semantics=("arbitrary",), vmem_limit_bytes=VMEM_LIMIT_BYTES),
        name="ffn_half_step",
    )(x, pre_g, w1, w2, post_g, *[w for w, _ in cast])
    return out[0], out[1:]


def _interleave(slots, fillers):
    n = len(slots)
    done = 0
    for t, (first, second) in enumerate(slots):
        first()
        upto = (len(fillers) * (t + 1)) // n
        for f in fillers[done:upto]:
            f()
        done = upto
        second()


def _mixer_kernel(sinks_ref, x_ref, xn_ref, pre_g_ref, w_in_ref, ln_g_ref, ln_b_ref, sgu_w_ref, sgu_bt_ref,
                  w_a_ref, w_s_ref, w_o_ref, post_g_ref, *refs, layer, n_cast):
    cast_src, o_ref, cast_dst = refs[:n_cast], refs[n_cast], refs[n_cast + 1:2 * n_cast + 1]
    (h_ref, hn_ref, q_ref, kd_ref, vp_ref, u_ref, gv_ref, vn_ref, ga_ref, gb_ref, ya_ref, ys_ref,
     a_ref, mg_ref, y_ref, wg_ref, bias_ref) = refs[2 * n_cast + 1:]
    _cast_slabs(cast_src, cast_dst)
    step = pl.program_id(0)
    n_sub = MIX_ROWS // MIX_SUB
    n_chunk = D_MODEL // MIX_CHUNK

    @pl.when(step == 0)
    def _():
        hn_ref[...] = _rms_norm(x_ref[:MIX_SUB, :], pre_g_ref[...]).astype(BF16)
        kd_ref[:, :ATTN_BLOCK, :] = jnp.zeros((N_KV_HEADS, ATTN_BLOCK, LANES), BF16)
        vp_ref[:, :ATTN_BLOCK, :] = jnp.zeros((2 * N_KV_HEADS, ATTN_BLOCK, LANES), BF16)
        ti = lax.broadcasted_iota(jnp.int32, (SGU_CHUNK, SGU_CHUNK), 0)
        si = lax.broadcasted_iota(jnp.int32, (SGU_CHUNK, SGU_CHUNK), 1)
        for g in range(SGU_GROUPS):
            wg_ref[g] = jnp.where(si <= ti, sgu_w_ref[g], 0.0).astype(BF16)
            bias_ref[g] = jnp.broadcast_to(sgu_bt_ref[:, g:g + 1], (SGU_CHUNK, SGU_GROUP_CH))

    qi = lax.broadcasted_iota(jnp.int32, (ATTN_BLOCK, ATTN_BLOCK), 0)
    kj = lax.broadcasted_iota(jnp.int32, (ATTN_BLOCK, ATTN_BLOCK), 1)
    cur_ok = kj <= qi
    lane_lo = kj < HEAD_DIM
    low = lax.broadcasted_iota(jnp.int32, (MIX_SUB, LANES), 1) < HEAD_DIM

    def proj_items(j):
        r0 = j * MIX_SUB
        rs = slice(r0, r0 + MIX_SUB)
        ks = slice(ATTN_BLOCK + r0, ATTN_BLOCK + r0 + MIX_SUB)
        items = []

        def norm():
            if j > 0:
                h_ref[rs, :] = _rms_norm(x_ref[rs, :], pre_g_ref[...]).astype(BF16)
        items.append(norm)

        def col(off, c):
            h = h_ref[rs, :] if j > 0 else hn_ref[...]
            return _dot(h, w_in_ref[:, off + c * MIX_CHUNK:off + (c + 1) * MIX_CHUNK])

        def q_item(c):
            def f():
                q_ref[rs, c * MIX_CHUNK:(c + 1) * MIX_CHUNK] = (
                    col(OFF_Q, c) * (1.0 / math.sqrt(HEAD_DIM))).astype(BF16)
            return f
        items += [q_item(c) for c in range(ATTN_WIDTH // MIX_CHUNK)]

        def kv_item():
            kv = col(OFF_KV, 0)
            k = kv[:, :KV_WIDTH]
            v = kv[:, KV_WIDTH:]
            k_sw = pltpu.roll(k, HEAD_DIM, 1)
            v_sw = pltpu.roll(v, HEAD_DIM, 1)
            zero = jnp.zeros_like(v)
            kd_ref[0, ks, :] = jnp.where(low, k, k_sw).astype(BF16)
            kd_ref[1, ks, :] = jnp.where(low, k_sw, k).astype(BF16)
            vp_ref[0, ks, :] = jnp.where(low, v, zero).astype(BF16)
            vp_ref[1, ks, :] = jnp.where(low, zero, v_sw).astype(BF16)
            vp_ref[2, ks, :] = jnp.where(low, v_sw, zero).astype(BF16)
            vp_ref[3, ks, :] = jnp.where(low, zero, v).astype(BF16)
        items.append(kv_item)

        def raw_item(dst_ref, dst_rows, off, c):
            def f():
                dst_ref[dst_rows, c * MIX_CHUNK:(c + 1) * MIX_CHUNK] = col(off, c)
            return f
        items += [raw_item(u_ref, rs, OFF_U, c) for c in range(n_chunk)]
        items += [raw_item(gv_ref, slice(0, MIX_SUB), OFF_V, c) for c in range(n_chunk)]

        def layer_norm():
            gv = _gelu(gv_ref[...])
            gc = gv - jnp.mean(gv, axis=-1, keepdims=True)
            vn = gc * lax.rsqrt(jnp.mean(gc * gc, axis=-1, keepdims=True) + LN_EPS)
            vn_ref[rs, :] = (vn * ln_g_ref[...] + ln_b_ref[...]).astype(BF16)
        items.append(layer_norm)
        items += [raw_item(ga_ref, rs, OFF_GA, c) for c in range(n_chunk)]
        items += [raw_item(gb_ref, rs, OFF_GB, c) for c in range(n_chunk)]
        return items

    def attn_slots(j):
        state = {}

        def scores(bi, head):
            def f():
                r0 = bi * ATTN_BLOCK
                g, hp, par = head // Q_PER_KV, head // 2, head % 2
                qp = q_ref[r0:r0 + ATTN_BLOCK, hp * LANES:(hp + 1) * LANES]
                qm = jnp.where(lane_lo if par == 0 else jnp.logical_not(lane_lo), qp, jnp.zeros_like(qp))
                s = _dot_nt(qm, kd_ref[g, r0:r0 + 2 * ATTN_BLOCK, :])
                s_prev = s[:, :ATTN_BLOCK]
                if bi == 0:
                    s_prev = jnp.where(step > 0, s_prev, MASK_VALUE)
                sm = jnp.where(cur_ok, s[:, ATTN_BLOCK:], s_prev)
                sink = sinks_ref[layer, head]
                m = jnp.maximum(jnp.max(sm, axis=-1, keepdims=True), sink)
                p = jnp.exp(sm - m)
                den = jnp.sum(p, axis=-1, keepdims=True) + jnp.exp(sink - m)
                pb = p.astype(BF16)
                zb = jnp.zeros_like(pb)
                state[(bi, head)] = (
                    jnp.concatenate([jnp.where(cur_ok, zb, pb), jnp.where(cur_ok, pb, zb)], axis=1), den)
            return f

        def weighted_sum(bi, head):
            def f():
                r0 = bi * ATTN_BLOCK
                g, hp, par = head // Q_PER_KV, head // 2, head % 2
                p2, den = state.pop((bi, head))
                o = _dot(p2, vp_ref[2 * g + par, r0:r0 + 2 * ATTN_BLOCK, :]) / den
                if par == 0:
                    state["acc"] = o
                else:
                    ya_ref[r0:r0 + ATTN_BLOCK, hp * LANES:(hp + 1) * LANES] = (
                        state.pop("acc") + o).astype(BF16)
            return f

        work = [(2 * j + b, head) for b in range(MIX_SUB // ATTN_BLOCK) for head in range(N_Q_HEADS)]
        nothing = lambda: None
        slots = []
        for t in range(len(work) + ATTN_SKEW):
            first = scores(*work[t]) if t < len(work) else nothing
            second = weighted_sum(*work[t - ATTN_SKEW]) if t >= ATTN_SKEW else nothing
            slots.append((first, second))
        return slots

    def out_items(j):
        r0 = j * MIX_SUB
        rs = slice(r0, r0 + MIX_SUB)
        ra = slice(r0, r0 + SGU_CHUNK)
        rb = slice(r0 + SGU_CHUNK, r0 + 2 * SGU_CHUNK)
        items = []

        def sgu_item(g):
            def f():
                cg = slice(g * SGU_GROUP_CH, (g + 1) * SGU_GROUP_CH)
                rhs = jnp.concatenate([vn_ref[ra, cg], vn_ref[rb, cg]], axis=1)
                sg = _dot(wg_ref[g], rhs)
                bias = bias_ref[g]
                ys_ref[ra, cg] = (_gelu(u_ref[ra, cg]) * (sg[:, :SGU_GROUP_CH] + bias)).astype(BF16)
                ys_ref[rb, cg] = (_gelu(u_ref[rb, cg]) * (sg[:, SGU_GROUP_CH:] + bias)).astype(BF16)
            return f

        def attn_branch_item(c):
            def f():
                cs = slice(c * MIX_CHUNK, (c + 1) * MIX_CHUNK)
                a_ref[:, cs] = jax.nn.sigmoid(ga_ref[rs, cs]) * _dot(ya_ref[rs, :], w_a_ref[:, cs])
            return f

        sgu_per_dot = SGU_GROUPS // n_chunk
        for c in range(n_chunk):
            items.append(attn_branch_item(c))
            items += [sgu_item(g) for g in range(c * sgu_per_dot, (c + 1) * sgu_per_dot)]

        def merge_item(c):
            def f():
                cs = slice(c * MIX_CHUNK, (c + 1) * MIX_CHUNK)
                s = _dot(ys_ref[rs, :], w_s_ref[:, cs])
                mg_ref[:, cs] = (a_ref[:, cs] + jax.nn.sigmoid(gb_ref[rs, cs]) * s).astype(BF16)
            return f
        items += [merge_item(c) for c in range(n_chunk)]

        def proj_item(c):
            def f():
                cs = slice(c * MIX_CHUNK, (c + 1) * MIX_CHUNK)
                y_ref[:, cs] = _dot(mg_ref[...], w_o_ref[:, cs])
            return f
        items += [proj_item(c) for c in range(n_chunk)]

        def finish():
            o_ref[rs, :] = x_ref[rs, :] + _rms_norm(y_ref[...], post_g_ref[...])
        items.append(finish)
        return items

    n_early = len(proj_items(0)) - n_chunk

    def norm_next():
        hn_ref[...] = _rms_norm(xn_ref[...], pre_g_ref[...]).astype(BF16)

    for stage in range(n_sub + 2):
        fillers = []
        if 1 <= stage <= n_sub:
            fillers += proj_items(stage - 1)[n_early:]
        if stage < n_sub:
            fillers += proj_items(stage)[:n_early]
        if stage >= 2:
            fillers += out_items(stage - 2)
        if stage == n_sub + 1:
            fillers.insert(1, norm_next)
        if 1 <= stage <= n_sub:
            _interleave(attn_slots(stage - 1), fillers)
        else:
            for f in fillers:
                f()

    kd_ref[:, :ATTN_BLOCK, :] = kd_ref[:, MIX_ROWS:MIX_ROWS + ATTN_BLOCK, :]
    vp_ref[:, :ATTN_BLOCK, :] = vp_ref[:, MIX_ROWS:MIX_ROWS + ATTN_BLOCK, :]


def _mixer(x, sinks, pre_g, w_in, ln_g, ln_b, sgu_w, sgu_bt, w_a, w_s, w_o, post_g, layer, cast):
    seq = x.shape[0]
    rows = MIX_ROWS
    steps = seq // rows
    row = lambda i: (i, 0)
    lay2 = lambda i: (layer, 0, 0)
    lay3 = lambda i: (layer, 0, 0, 0)
    whole = lambda i: (0, 0)
    next_sub = lambda i: (jnp.minimum((i + 1) * (rows // MIX_SUB), seq // MIX_SUB - 1), 0)
    cast_in, cast_out, cast_shapes = _cast_specs(cast, steps)
    out = pl.pallas_call(
        functools.partial(_mixer_kernel, layer=layer, n_cast=len(cast)),
        grid=(steps,),
        in_specs=[
            pl.BlockSpec(memory_space=pltpu.SMEM),
            pl.BlockSpec((rows, D_MODEL), row),
            pl.BlockSpec((MIX_SUB, D_MODEL), next_sub),
            _resident((None, 1, D_MODEL), lay2),
            _resident((D_MODEL, IN_WIDTH), whole),
            _resident((None, 1, SGU_WIDTH), lay2),
            _resident((None, 1, SGU_WIDTH), lay2),
            _resident((None, SGU_GROUPS, SGU_CHUNK, SGU_CHUNK), lay3),
            _resident((None, SGU_CHUNK, SGU_GROUPS), lay2),
            _resident((ATTN_WIDTH, D_MODEL), whole),
            _resident((SGU_WIDTH, D_MODEL), whole),
            _resident((D_MODEL, D_MODEL), whole),
            _resident((None, 1, D_MODEL), lay2),
        ] + cast_in,
        out_specs=[pl.BlockSpec((rows, D_MODEL), row)] + cast_out,
        out_shape=[jax.ShapeDtypeStruct(x.shape, x.dtype)] + cast_shapes,
        scratch_shapes=[
            pltpu.VMEM((rows, D_MODEL), BF16),
            pltpu.VMEM((MIX_SUB, D_MODEL), BF16),
            pltpu.VMEM((rows, ATTN_WIDTH), BF16),
            pltpu.VMEM((N_KV_HEADS, rows + ATTN_BLOCK, LANES), BF16),
            pltpu.VMEM((2 * N_KV_HEADS, rows + ATTN_BLOCK, LANES), BF16),
            pltpu.VMEM((rows, SGU_WIDTH), F32),
            pltpu.VMEM((MIX_SUB, SGU_WIDTH), F32),
            pltpu.VMEM((rows, SGU_WIDTH), BF16),
            pltpu.VMEM((rows, D_MODEL), F32),
            pltpu.VMEM((rows, D_MODEL), F32),
            pltpu.VMEM((rows, ATTN_WIDTH), BF16),
            pltpu.VMEM((rows, SGU_WIDTH), BF16),
            pltpu.VMEM((MIX_SUB, D_MODEL), F32),
            pltpu.VMEM((MIX_SUB, D_MODEL), BF16),
            pltpu.VMEM((MIX_SUB, D_MODEL), F32),
            pltpu.VMEM((SGU_GROUPS, SGU_CHUNK, SGU_CHUNK), BF16),
            pltpu.VMEM((SGU_GROUPS, SGU_CHUNK, SGU_GROUP_CH), F32),
        ],
        compiler_params=pltpu.CompilerParams(
            dimension_semantics=("arbitrary",), vmem_limit_bytes=VMEM_LIMIT_BYTES),
        name="token_mixer",
    )(sinks, x, x, pre_g, w_in, ln_g, ln_b, sgu_w, sgu_bt, w_a, w_s, w_o, post_g, *[w for w, _ in cast])
    return out[0], out[1:]


def kernel(x, ffn1_pre_g, ffn1_w1, ffn1_w2, ffn1_post_g, mix_pre_g, w_in, attn_sinks, sgu_ln_g, sgu_ln_b,
           sgu_w, sgu_b, w_attn_branch, w_sgu_branch, w_out, mix_post_g, ffn2_pre_g, ffn2_w1, ffn2_w2,
           ffn2_post_g):
    batch, seq, d = x.shape
    depth = w_in.shape[0]
    assert batch == 1 and d == D_MODEL and seq % MIX_ROWS == 0 and seq % FFN_ROWS == 0
    vec = lambda g: g.reshape(depth, 1, -1)
    sgu_bt = jnp.swapaxes(sgu_b, 1, 2)
    h = x.reshape(seq, d)
    f1w1, f1w2 = ffn1_w1[0].astype(BF16), ffn1_w2[0].astype(BF16)
    for l in range(depth):
        h, (w_in_b, w_a, w_s, w_o, f2w2) = _ffn(
            h, vec(ffn1_pre_g), f1w1, f1w2, vec(ffn1_post_g), l,
            [(w_in, l), (w_attn_branch, l), (w_sgu_branch, l), (w_out, l), (ffn2_w2, l)])
        h, (f2w1,) = _mixer(h, attn_sinks, vec(mix_pre_g), w_in_b, vec(sgu_ln_g), vec(sgu_ln_b), sgu_w, sgu_bt,
                            w_a, w_s, w_o, vec(mix_post_g), l, [(ffn2_w1, l)])
        nxt = [(ffn1_w1, l + 1), (ffn1_w2, l + 1)] if l + 1 < depth else []
        h, nxt_w = _ffn(h, vec(ffn2_pre_g), f2w1, f2w2, vec(ffn2_post_g), l, nxt)
        if nxt:
            f1w1, f1w2 = nxt_w
    return h.reshape(batch, seq, d)
```

```python
import functools
import math

import jax
import jax.numpy as jnp
from jax import lax
from jax.experimental import pallas as pl
from jax.experimental.pallas import tpu as pltpu

D_MODEL = 1024
D_FF = 2816
HEAD_DIM = 64
N_Q_HEADS = 16
N_KV_HEADS = 2
Q_PER_KV = N_Q_HEADS // N_KV_HEADS
ATTN_WIDTH = N_Q_HEADS * HEAD_DIM
KV_WIDTH = N_KV_HEADS * HEAD_DIM
ATTN_BLOCK = 128
SGU_CHUNK = 128
SGU_GROUPS = 8
SGU_GROUP_CH = 128
SGU_WIDTH = SGU_GROUPS * SGU_GROUP_CH
OFF_Q = 0
OFF_KV = ATTN_WIDTH
OFF_U = OFF_KV + 2 * KV_WIDTH
OFF_V = OFF_U + SGU_WIDTH
OFF_GA = OFF_V + SGU_WIDTH
OFF_GB = OFF_GA + D_MODEL
IN_WIDTH = OFF_GB + D_MODEL

RMS_EPS = 1e-6
LN_EPS = 1e-5
MASK_VALUE = -1e30

LANES = 128
BF16_SUBLANES = 16
MXU_COLS = 256
VMEM_LIMIT_BYTES = 58 * 1024 * 1024

FFN_ROWS = 1024
FFN_SUB = 512
FFN_CHUNK = MXU_COLS
MIX_ROWS = 512
MIX_SUB = 2 * ATTN_BLOCK
MIX_CHUNK = MXU_COLS
ATTN_SKEW = 1

F32 = jnp.float32
BF16 = jnp.bfloat16


def _rms_norm(x, g):
    return x * lax.rsqrt(jnp.mean(x * x, axis=-1, keepdims=True) + RMS_EPS) * g


def _gelu(x):
    return 0.5 * x * (1.0 + lax.erf(x * math.sqrt(0.5)))


def _dot(a, b):
    return jnp.dot(a, b, preferred_element_type=F32)


def _dot_nt(a, b):
    return lax.dot_general(a, b, (((1,), (1,)), ((), ())), preferred_element_type=F32)


def _cast_slabs(src_refs, dst_refs):
    for src, dst in zip(src_refs, dst_refs):
        dst[...] = src[...].astype(BF16)


def _ffn_kernel(x_ref, pre_g_ref, w1_ref, w2_ref, post_g_ref, *refs, n_cast):
    cast_src, o_ref, cast_dst = refs[:n_cast], refs[n_cast], refs[n_cast + 1:2 * n_cast + 1]
    h_ref, act_ref, y_ref = refs[2 * n_cast + 1:]
    _cast_slabs(cast_src, cast_dst)
    n_sub = FFN_ROWS // FFN_SUB

    def rows(j):
        return slice(j * FFN_SUB, (j + 1) * FFN_SUB)

    def norm(j):
        h_ref[rows(j), :] = _rms_norm(x_ref[rows(j), :], pre_g_ref[...]).astype(BF16)

    def finish(j):
        o_ref[rows(j), :] = x_ref[rows(j), :] + 0.5 * _rms_norm(y_ref[rows(j), :], post_g_ref[...])

    def down(j):
        y_ref[rows(j), :] = _dot(act_ref[rows(j), :], w2_ref[...])

    norm(0)
    for j in range(n_sub):
        for c in range(D_FF // FFN_CHUNK):
            lo = c * FFN_CHUNK
            g = _dot(h_ref[rows(j), :], w1_ref[:, lo:lo + FFN_CHUNK])
            u = _dot(h_ref[rows(j), :], w1_ref[:, D_FF + lo:D_FF + lo + FFN_CHUNK])
            act_ref[rows(j), lo:lo + FFN_CHUNK] = (jax.nn.silu(g) * u).astype(BF16)
            if c == 0 and j + 1 < n_sub:
                norm(j + 1)
            if c == 1 and j > 0:
                down(j - 1)
            if c == 3 and j > 0:
                finish(j - 1)
    down(n_sub - 1)
    finish(n_sub - 1)


def _resident(block_shape, index_map):
    return pl.BlockSpec(block_shape, index_map, pipeline_mode=pl.Buffered(1))


def _cast_specs(cast, steps):
    in_specs, out_specs, out_shapes = [], [], []
    for w, layer in cast:
        _, r, c = w.shape
        span = 1
        while (r // (steps // span)) % BF16_SUBLANES:
            span *= 2
        slab = r // (steps // span)
        assert slab * (steps // span) == r
        in_specs.append(pl.BlockSpec((None, slab, c), lambda i, layer=layer, span=span: (layer, i // span, 0)))
        out_specs.append(pl.BlockSpec((slab, c), lambda i, span=span: (i // span, 0)))
        out_shapes.append(jax.ShapeDtypeStruct((r, c), BF16))
    return in_specs, out_specs, out_shapes


def _ffn(x, pre_g, w1, w2, post_g, layer, cast):
    seq = x.shape[0]
    steps = seq // FFN_ROWS
    row = lambda i: (i, 0)
    lay2 = lambda i: (layer, 0, 0)
    whole = lambda i: (0, 0)
    cast_in, cast_out, cast_shapes = _cast_specs(cast, steps)
    out = pl.pallas_call(
        functools.partial(_ffn_kernel, n_cast=len(cast)),
        grid=(steps,),
        in_specs=[
            pl.BlockSpec((FFN_ROWS, D_MODEL), row),
            _resident((None, 1, D_MODEL), lay2),
            _resident((D_MODEL, 2 * D_FF), whole),
            _resident((D_FF, D_MODEL), whole),
            _resident((None, 1, D_MODEL), lay2),
        ] + cast_in,
        out_specs=[pl.BlockSpec((FFN_ROWS, D_MODEL), row)] + cast_out,
        out_shape=[jax.ShapeDtypeStruct(x.shape, x.dtype)] + cast_shapes,
        scratch_shapes=[
            pltpu.VMEM((FFN_ROWS, D_MODEL), BF16),
            pltpu.VMEM((FFN_ROWS, D_FF), BF16),
            pltpu.VMEM((FFN_ROWS, D_MODEL), F32),
        ],
        compiler_params=pltpu.CompilerParams(
            dimension_semantics=("arbitrary",), vmem_limit_bytes=VMEM_LIMIT_BYTES),
        name="ffn_half_step",
    )(x, pre_g, w1, w2, post_g, *[w for w, _ in cast])
    return out[0], out[1:]


def _interleave(slots, fillers):
    n = len(slots)
    done = 0
    for t, (first, second) in enumerate(slots):
        first()
        upto = (len(fillers) * (t + 1)) // n
        for f in fillers[done:upto]:
            f()
        done = upto
        second()


def _mixer_kernel(sinks_ref, x_ref, pre_g_ref, w_in_ref, ln_g_ref, ln_b_ref, sgu_w_ref, sgu_bt_ref,
                  w_a_ref, w_s_ref, w_o_ref, post_g_ref, *refs, layer, n_cast):
    cast_src, o_ref, cast_dst = refs[:n_cast], refs[n_cast], refs[n_cast + 1:2 * n_cast + 1]
    (h_ref, q_ref, kd_ref, vt_ref, u_ref, gv_ref, vn_ref, ga_ref, gb_ref, ya_ref, ys_ref,
     a_ref, mg_ref, y_ref, wg_ref, bias_ref) = refs[2 * n_cast + 1:]
    _cast_slabs(cast_src, cast_dst)
    step = pl.program_id(0)
    n_sub = MIX_ROWS // MIX_SUB
    n_chunk = D_MODEL // MIX_CHUNK

    @pl.when(step == 0)
    def _():
        kd_ref[:, :ATTN_BLOCK, :] = jnp.zeros((N_KV_HEADS, ATTN_BLOCK, LANES), BF16)
        vt_ref[:, :ATTN_BLOCK] = jnp.zeros((KV_WIDTH, ATTN_BLOCK), BF16)
        ti = lax.broadcasted_iota(jnp.int32, (SGU_CHUNK, SGU_CHUNK), 0)
        si = lax.broadcasted_iota(jnp.int32, (SGU_CHUNK, SGU_CHUNK), 1)
        for g in range(SGU_GROUPS):
            wg_ref[g] = jnp.where(si <= ti, sgu_w_ref[g], 0.0).astype(BF16)
            bias_ref[g] = jnp.broadcast_to(sgu_bt_ref[:, g:g + 1], (SGU_CHUNK, SGU_GROUP_CH))

    key_l = lax.broadcasted_iota(jnp.int32, (ATTN_BLOCK, 2 * ATTN_BLOCK), 0)
    qry_c = lax.broadcasted_iota(jnp.int32, (ATTN_BLOCK, 2 * ATTN_BLOCK), 1)
    cur_ok = key_l <= (qry_c & (ATTN_BLOCK - 1))
    odd_head = lax.broadcasted_iota(jnp.int32, (1, 2 * ATTN_BLOCK), 1) >= ATTN_BLOCK
    lane_lo = lax.broadcasted_iota(jnp.int32, (ATTN_BLOCK, LANES), 1) < HEAD_DIM
    low = lax.broadcasted_iota(jnp.int32, (MIX_SUB, LANES), 1) < HEAD_DIM

    def proj_items(j):
        r0 = j * MIX_SUB
        rs = slice(r0, r0 + MIX_SUB)
        ks = slice(ATTN_BLOCK + r0, ATTN_BLOCK + r0 + MIX_SUB)
        items = []

        def norm():
            h_ref[rs, :] = _rms_norm(x_ref[rs, :], pre_g_ref[...]).astype(BF16)
        items.append(norm)

        def col(off, c):
            return _dot(h_ref[rs, :], w_in_ref[:, off + c * MIX_CHUNK:off + (c + 1) * MIX_CHUNK])

        def q_item(c):
            def f():
                q_ref[rs, c * MIX_CHUNK:(c + 1) * MIX_CHUNK] = (
                    col(OFF_Q, c) * (1.0 / math.sqrt(HEAD_DIM))).astype(BF16)
            return f
        items += [q_item(c) for c in range(ATTN_WIDTH // MIX_CHUNK)]

        def kv_item():
            kv = col(OFF_KV, 0)
            k = kv[:, :KV_WIDTH]
            v = kv[:, KV_WIDTH:]
            k_sw = pltpu.roll(k, HEAD_DIM, 1)
            kd_ref[0, ks, :] = jnp.where(low, k, k_sw).astype(BF16)
            kd_ref[1, ks, :] = jnp.where(low, k_sw, k).astype(BF16)
            vt_ref[:, ks] = v.T.astype(BF16)
        items.append(kv_item)

        def raw_item(dst_ref, dst_rows, off, c):
            def f():
                dst_ref[dst_rows, c * MIX_CHUNK:(c + 1) * MIX_CHUNK] = col(off, c)
            return f
        items += [raw_item(u_ref, rs, OFF_U, c) for c in range(n_chunk)]
        items += [raw_item(gv_ref, slice(0, MIX_SUB), OFF_V, c) for c in range(n_chunk)]

        def layer_norm():
            gv = _gelu(gv_ref[...])
            gc = gv - jnp.mean(gv, axis=-1, keepdims=True)
            vn = gc * lax.rsqrt(jnp.mean(gc * gc, axis=-1, keepdims=True) + LN_EPS)
            vn_ref[rs, :] = (vn * ln_g_ref[...] + ln_b_ref[...]).astype(BF16)
        items.append(layer_norm)
        items += [raw_item(ga_ref, rs, OFF_GA, c) for c in range(n_chunk)]
        items += [raw_item(gb_ref, rs, OFF_GB, c) for c in range(n_chunk)]
        return items

    def attn_slots(j):
        state = {}

        def scores(bi, hp):
            def f():
                r0 = bi * ATTN_BLOCK
                g = (2 * hp) // Q_PER_KV
                qp = q_ref[r0:r0 + ATTN_BLOCK, hp * LANES:(hp + 1) * LANES]
                zq = jnp.zeros_like(qp)
                qcat = jnp.concatenate([jnp.where(lane_lo, qp, zq), jnp.where(lane_lo, zq, qp)], axis=0)
                s = _dot_nt(kd_ref[g, r0:r0 + 2 * ATTN_BLOCK, :], qcat)
                s_prev = s[:ATTN_BLOCK, :]
                if bi == 0:
                    s_prev = jnp.where(step > 0, s_prev, MASK_VALUE)
                sm = jnp.where(cur_ok, s[ATTN_BLOCK:, :], s_prev)
                sink = jnp.where(odd_head, sinks_ref[layer, 2 * hp + 1], sinks_ref[layer, 2 * hp])
                m = jnp.maximum(jnp.max(sm, axis=0, keepdims=True), sink)
                p = jnp.exp(sm - m)
                den = jnp.sum(p, axis=0, keepdims=True) + jnp.exp(sink - m)
                pb = p.astype(BF16)
                zb = jnp.zeros_like(pb)
                state[(bi, hp)] = (
                    jnp.concatenate([jnp.where(cur_ok, zb, pb), jnp.where(cur_ok, pb, zb)], axis=0), den)
            return f

        def weighted_sum(bi, hp):
            def f():
                r0 = bi * ATTN_BLOCK
                g = (2 * hp) // Q_PER_KV
                p2, den = state.pop((bi, hp))
                vt = vt_ref[g * HEAD_DIM:(g + 1) * HEAD_DIM, r0:r0 + 2 * ATTN_BLOCK]
                o = _dot(vt, p2) / den
                pair = jnp.concatenate([o[:, :ATTN_BLOCK], o[:, ATTN_BLOCK:]], axis=0)
                ya_ref[r0:r0 + ATTN_BLOCK, hp * LANES:(hp + 1) * LANES] = pair.T.astype(BF16)
            return f

        work = [(2 * j + b, hp) for b in range(MIX_SUB // ATTN_BLOCK) for hp in range(N_Q_HEADS // 2)]
        nothing = lambda: None
        slots = []
        for t in range(len(work) + ATTN_SKEW):
            first = scores(*work[t]) if t < len(work) else nothing
            second = weighted_sum(*work[t - ATTN_SKEW]) if t >= ATTN_SKEW else nothing
            slots.append((first, second))
        return slots

    def out_items(j):
        r0 = j * MIX_SUB
        rs = slice(r0, r0 + MIX_SUB)
        ra = slice(r0, r0 + SGU_CHUNK)
        rb = slice(r0 + SGU_CHUNK, r0 + 2 * SGU_CHUNK)
        items = []

        def sgu_item(g):
            def f():
                cg = slice(g * SGU_GROUP_CH, (g + 1) * SGU_GROUP_CH)
                rhs = jnp.concatenate([vn_ref[ra, cg], vn_ref[rb, cg]], axis=1)
                sg = _dot(wg_ref[g], rhs)
                bias = bias_ref[g]
                ys_ref[ra, cg] = (_gelu(u_ref[ra, cg]) * (sg[:, :SGU_GROUP_CH] + bias)).astype(BF16)
                ys_ref[rb, cg] = (_gelu(u_ref[rb, cg]) * (sg[:, SGU_GROUP_CH:] + bias)).astype(BF16)
            return f

        def attn_branch_item(c):
            def f():
                cs = slice(c * MIX_CHUNK, (c + 1) * MIX_CHUNK)
                a_ref[:, cs] = jax.nn.sigmoid(ga_ref[rs, cs]) * _dot(ya_ref[rs, :], w_a_ref[:, cs])
            return f

        sgu_per_dot = SGU_GROUPS // n_chunk
        for c in range(n_chunk):
            items.append(attn_branch_item(c))
            items += [sgu_item(g) for g in range(c * sgu_per_dot, (c + 1) * sgu_per_dot)]

        def merge_item(c):
            def f():
                cs = slice(c * MIX_CHUNK, (c + 1) * MIX_CHUNK)
                s = _dot(ys_ref[rs, :], w_s_ref[:, cs])
                mg_ref[:, cs] = (a_ref[:, cs] + jax.nn.sigmoid(gb_ref[rs, cs]) * s).astype(BF16)
            return f
        items += [merge_item(c) for c in range(n_chunk)]

        def proj_item(c):
            def f():
                cs = slice(c * MIX_CHUNK, (c + 1) * MIX_CHUNK)
                y_ref[:, cs] = _dot(mg_ref[...], w_o_ref[:, cs])
            return f
        items += [proj_item(c) for c in range(n_chunk)]

        def finish():
            o_ref[rs, :] = x_ref[rs, :] + _rms_norm(y_ref[...], post_g_ref[...])
        items.append(finish)
        return items

    n_early = len(proj_items(0)) - n_chunk
    for stage in range(n_sub + 2):
        fillers = []
        if 1 <= stage <= n_sub:
            fillers += proj_items(stage - 1)[n_early:]
        if stage < n_sub:
            fillers += proj_items(stage)[:n_early]
        if stage >= 2:
            fillers += out_items(stage - 2)
        if 1 <= stage <= n_sub:
            _interleave(attn_slots(stage - 1), fillers)
        else:
            for f in fillers:
                f()

    kd_ref[:, :ATTN_BLOCK, :] = kd_ref[:, MIX_ROWS:MIX_ROWS + ATTN_BLOCK, :]
    vt_ref[:, :ATTN_BLOCK] = vt_ref[:, MIX_ROWS:MIX_ROWS + ATTN_BLOCK]


def _mixer(x, sinks, pre_g, w_in, ln_g, ln_b, sgu_w, sgu_bt, w_a, w_s, w_o, post_g, layer, cast):
    seq = x.shape[0]
    rows = MIX_ROWS
    steps = seq // rows
    row = lambda i: (i, 0)
    lay2 = lambda i: (layer, 0, 0)
    lay3 = lambda i: (layer, 0, 0, 0)
    whole = lambda i: (0, 0)
    cast_in, cast_out, cast_shapes = _cast_specs(cast, steps)
    out = pl.pallas_call(
        functools.partial(_mixer_kernel, layer=layer, n_cast=len(cast)),
        grid=(steps,),
        in_specs=[
            pl.BlockSpec(memory_space=pltpu.SMEM),
            pl.BlockSpec((rows, D_MODEL), row),
            _resident((None, 1, D_MODEL), lay2),
            _resident((D_MODEL, IN_WIDTH), whole),
            _resident((None, 1, SGU_WIDTH), lay2),
            _resident((None, 1, SGU_WIDTH), lay2),
            _resident((None, SGU_GROUPS, SGU_CHUNK, SGU_CHUNK), lay3),
            _resident((None, SGU_CHUNK, SGU_GROUPS), lay2),
            _resident((ATTN_WIDTH, D_MODEL), whole),
            _resident((SGU_WIDTH, D_MODEL), whole),
            _resident((D_MODEL, D_MODEL), whole),
            _resident((None, 1, D_MODEL), lay2),
        ] + cast_in,
        out_specs=[pl.BlockSpec((rows, D_MODEL), row)] + cast_out,
        out_shape=[jax.ShapeDtypeStruct(x.shape, x.dtype)] + cast_shapes,
        scratch_shapes=[
            pltpu.VMEM((rows, D_MODEL), BF16),
            pltpu.VMEM((rows, ATTN_WIDTH), BF16),
            pltpu.VMEM((N_KV_HEADS, rows + ATTN_BLOCK, LANES), BF16),
            pltpu.VMEM((KV_WIDTH, rows + ATTN_BLOCK), BF16),
            pltpu.VMEM((rows, SGU_WIDTH), F32),
            pltpu.VMEM((MIX_SUB, SGU_WIDTH), F32),
            pltpu.VMEM((rows, SGU_WIDTH), BF16),
            pltpu.VMEM((rows, D_MODEL), F32),
            pltpu.VMEM((rows, D_MODEL), F32),
            pltpu.VMEM((rows, ATTN_WIDTH), BF16),
            pltpu.VMEM((rows, SGU_WIDTH), BF16),
            pltpu.VMEM((MIX_SUB, D_MODEL), F32),
            pltpu.VMEM((MIX_SUB, D_MODEL), BF16),
            pltpu.VMEM((MIX_SUB, D_MODEL), F32),
            pltpu.VMEM((SGU_GROUPS, SGU_CHUNK, SGU_CHUNK), BF16),
            pltpu.VMEM((SGU_GROUPS, SGU_CHUNK, SGU_GROUP_CH), F32),
        ],
        compiler_params=pltpu.CompilerParams(
            dimension_semantics=("arbitrary",), vmem_limit_bytes=VMEM_LIMIT_BYTES),
        name="token_mixer",
    )(sinks, x, pre_g, w_in, ln_g, ln_b, sgu_w, sgu_bt, w_a, w_s, w_o, post_g, *[w for w, _ in cast])
    return out[0], out[1:]


def kernel(x, ffn1_pre_g, ffn1_w1, ffn1_w2, ffn1_post_g, mix_pre_g, w_in, attn_sinks, sgu_ln_g, sgu_ln_b,
           sgu_w, sgu_b, w_attn_branch, w_sgu_branch, w_out, mix_post_g, ffn2_pre_g, ffn2_w1, ffn2_w2,
           ffn2_post_g):
    batch, seq, d = x.shape
    depth = w_in.shape[0]
    assert batch == 1 and d == D_MODEL and seq % MIX_ROWS == 0 and seq % FFN_ROWS == 0
    vec = lambda g: g.reshape(depth, 1, -1)
    sgu_bt = jnp.swapaxes(sgu_b, 1, 2)
    h = x.reshape(seq, d)
    f1w1, f1w2 = ffn1_w1[0].astype(BF16), ffn1_w2[0].astype(BF16)
    for l in range(depth):
        h, (w_in_b, w_a, w_s, w_o, f2w2) = _ffn(
            h, vec(ffn1_pre_g), f1w1, f1w2, vec(ffn1_post_g), l,
            [(w_in, l), (w_attn_branch, l), (w_sgu_branch, l), (w_out, l), (ffn2_w2, l)])
        h, (f2w1,) = _mixer(h, attn_sinks, vec(mix_pre_g), w_in_b, vec(sgu_ln_g), vec(sgu_ln_b), sgu_w, sgu_bt,
                            w_a, w_s, w_o, vec(mix_post_g), l, [(ffn2_w1, l)])
        nxt = [(ffn1_w1, l + 1), (ffn1_w2, l + 1)] if l + 1 < depth else []
        h, nxt_w = _ffn(h, vec(ffn2_pre_g), f2w1, f2w2, vec(ffn2_post_g), l, nxt)
        if nxt:
            f1w1, f1w2 = nxt_w
    return h.reshape(batch, seq, d)
```

```python
import functools
import math

import jax
import jax.numpy as jnp
from jax import lax
from jax.experimental import pallas as pl
from jax.experimental.pallas import tpu as pltpu

D_MODEL = 1024
D_FF = 2816
HEAD_DIM = 64
N_Q_HEADS = 16
N_KV_HEADS = 2
Q_PER_KV = N_Q_HEADS // N_KV_HEADS
ATTN_WIDTH = N_Q_HEADS * HEAD_DIM
KV_WIDTH = N_KV_HEADS * HEAD_DIM
ATTN_BLOCK = 128
HALF = ATTN_BLOCK // 2
SGU_CHUNK = 128
SGU_GROUPS = 8
SGU_GROUP_CH = 128
SGU_WIDTH = SGU_GROUPS * SGU_GROUP_CH
OFF_Q = 0
OFF_KV = ATTN_WIDTH
OFF_U = OFF_KV + 2 * KV_WIDTH
OFF_V = OFF_U + SGU_WIDTH
OFF_GA = OFF_V + SGU_WIDTH
OFF_GB = OFF_GA + D_MODEL
IN_WIDTH = OFF_GB + D_MODEL

RMS_EPS = 1e-6
LN_EPS = 1e-5
MASK_VALUE = -1e30

LANES = 128
BF16_SUBLANES = 16
MXU_COLS = 256
VMEM_LIMIT_BYTES = 58 * 1024 * 1024

FFN_ROWS = 1024
FFN_SUB = 256
FFN_CHUNK = MXU_COLS
MIX_ROWS = 512
MIX_SUB = 2 * ATTN_BLOCK
MIX_CHUNK = MXU_COLS
ATTN_SKEW = 2

F32 = jnp.float32
BF16 = jnp.bfloat16


def _rms_norm(x, g):
    return x * lax.rsqrt(jnp.mean(x * x, axis=-1, keepdims=True) + RMS_EPS) * g


def _gelu(x):
    return 0.5 * x * (1.0 + lax.erf(x * math.sqrt(0.5)))


def _dot(a, b):
    return jnp.dot(a, b, preferred_element_type=F32)


def _dot_nt(a, b):
    return lax.dot_general(a, b, (((1,), (1,)), ((), ())), preferred_element_type=F32)


def _cast_slabs(src_refs, dst_refs):
    for src, dst in zip(src_refs, dst_refs):
        dst[...] = src[...].astype(BF16)


def _ffn_kernel(x_ref, pre_g_ref, w1_ref, w2_ref, post_g_ref, *refs, n_cast):
    cast_src, o_ref, cast_dst = refs[:n_cast], refs[n_cast], refs[n_cast + 1:2 * n_cast + 1]
    h_ref, act_ref, y_ref = refs[2 * n_cast + 1:]
    _cast_slabs(cast_src, cast_dst)
    n_sub = FFN_ROWS // FFN_SUB

    def rows(j):
        return slice(j * FFN_SUB, (j + 1) * FFN_SUB)

    def norm(j):
        h_ref[rows(j), :] = _rms_norm(x_ref[rows(j), :], pre_g_ref[...]).astype(BF16)

    def finish(j):
        o_ref[rows(j), :] = x_ref[rows(j), :] + 0.5 * _rms_norm(y_ref[rows(j), :], post_g_ref[...])

    def down(j):
        y_ref[rows(j), :] = _dot(act_ref[rows(j), :], w2_ref[...])

    norm(0)
    for j in range(n_sub):
        for c in range(D_FF // FFN_CHUNK):
            lo = c * FFN_CHUNK
            g = _dot(h_ref[rows(j), :], w1_ref[:, lo:lo + FFN_CHUNK])
            u = _dot(h_ref[rows(j), :], w1_ref[:, D_FF + lo:D_FF + lo + FFN_CHUNK])
            act_ref[rows(j), lo:lo + FFN_CHUNK] = (jax.nn.silu(g) * u).astype(BF16)
            if c == 0 and j + 1 < n_sub:
                norm(j + 1)
            if c == 1 and j > 0:
                down(j - 1)
            if c == 3 and j > 0:
                finish(j - 1)
    down(n_sub - 1)
    finish(n_sub - 1)


def _resident(block_shape, index_map):
    return pl.BlockSpec(block_shape, index_map, pipeline_mode=pl.Buffered(1))


def _cast_specs(cast, steps):
    in_specs, out_specs, out_shapes = [], [], []
    for w, layer in cast:
        _, r, c = w.shape
        span = 1
        while (r // (steps // span)) % BF16_SUBLANES:
            span *= 2
        slab = r // (steps // span)
        assert slab * (steps // span) == r
        in_specs.append(pl.BlockSpec((None, slab, c), lambda i, layer=layer, span=span: (layer, i // span, 0)))
        out_specs.append(pl.BlockSpec((slab, c), lambda i, span=span: (i // span, 0)))
        out_shapes.append(jax.ShapeDtypeStruct((r, c), BF16))
    return in_specs, out_specs, out_shapes


def _ffn(x, pre_g, w1, w2, post_g, layer, cast):
    seq = x.shape[0]
    steps = seq // FFN_ROWS
    row = lambda i: (i, 0)
    lay2 = lambda i: (layer, 0, 0)
    whole = lambda i: (0, 0)
    cast_in, cast_out, cast_shapes = _cast_specs(cast, steps)
    out = pl.pallas_call(
        functools.partial(_ffn_kernel, n_cast=len(cast)),
        grid=(steps,),
        in_specs=[
            pl.BlockSpec((FFN_ROWS, D_MODEL), row),
            _resident((None, 1, D_MODEL), lay2),
            _resident((D_MODEL, 2 * D_FF), whole),
            _resident((D_FF, D_MODEL), whole),
            _resident((None, 1, D_MODEL), lay2),
        ] + cast_in,
        out_specs=[pl.BlockSpec((FFN_ROWS, D_MODEL), row)] + cast_out,
        out_shape=[jax.ShapeDtypeStruct(x.shape, x.dtype)] + cast_shapes,
        scratch_shapes=[
            pltpu.VMEM((FFN_ROWS, D_MODEL), BF16),
            pltpu.VMEM((FFN_ROWS, D_FF), BF16),
            pltpu.VMEM((FFN_ROWS, D_MODEL), F32),
        ],
        compiler_params=pltpu.CompilerParams(
            dimension_semantics=("arbitrary",), vmem_limit_bytes=VMEM_LIMIT_BYTES),
        name="ffn_half_step",
    )(x, pre_g, w1, w2, post_g, *[w for w, _ in cast])
    return out[0], out[1:]


def _interleave(slots, fillers):
    n = len(slots)
    done = 0
    for t, (first, second) in enumerate(slots):
        first()
        upto = (len(fillers) * (t + 1)) // n
        for f in fillers[done:upto]:
            f()
        done = upto
        second()


def _mixer_kernel(sinks_ref, x_ref, pre_g_ref, w_in_ref, ln_g_ref, ln_b_ref, sgu_w_ref, sgu_bt_ref,
                  w_a_ref, w_s_ref, w_o_ref, post_g_ref, *refs, layer, n_cast):
    cast_src, o_ref, cast_dst = refs[:n_cast], refs[n_cast], refs[n_cast + 1:2 * n_cast + 1]
    (h_ref, q_ref, kd_ref, vt_ref, u_ref, gv_ref, vn_ref, ga_ref, gb_ref, ya_ref, ys_ref,
     a_ref, mg_ref, y_ref, wg_ref, bias_ref) = refs[2 * n_cast + 1:]
    _cast_slabs(cast_src, cast_dst)
    step = pl.program_id(0)
    n_sub = MIX_ROWS // MIX_SUB
    n_chunk = D_MODEL // MIX_CHUNK

    @pl.when(step == 0)
    def _():
        kd_ref[:, :ATTN_BLOCK, :] = jnp.zeros((N_KV_HEADS, ATTN_BLOCK, LANES), BF16)
        vt_ref[:, :ATTN_BLOCK] = jnp.zeros((KV_WIDTH, ATTN_BLOCK), BF16)
        ti = lax.broadcasted_iota(jnp.int32, (SGU_CHUNK, SGU_CHUNK), 0)
        si = lax.broadcasted_iota(jnp.int32, (SGU_CHUNK, SGU_CHUNK), 1)
        for g in range(SGU_GROUPS):
            wg_ref[g] = jnp.where(si <= ti, sgu_w_ref[g], 0.0).astype(BF16)
            bias_ref[g] = jnp.broadcast_to(sgu_bt_ref[:, g:g + 1], (SGU_CHUNK, SGU_GROUP_CH))

    key_l = lax.broadcasted_iota(jnp.int32, (HALF, 4 * HALF), 0)
    qry_c = lax.broadcasted_iota(jnp.int32, (HALF, 4 * HALF), 1)
    cur_ok = key_l <= (qry_c & (HALF - 1))
    head_slot = lax.broadcasted_iota(jnp.int32, (1, 4 * HALF), 1) // HALF
    lane_lo_h = lax.broadcasted_iota(jnp.int32, (HALF, LANES), 1) < HEAD_DIM
    low = lax.broadcasted_iota(jnp.int32, (MIX_SUB, LANES), 1) < HEAD_DIM

    def proj_items(j):
        r0 = j * MIX_SUB
        rs = slice(r0, r0 + MIX_SUB)
        ks = slice(ATTN_BLOCK + r0, ATTN_BLOCK + r0 + MIX_SUB)
        items = []

        def norm():
            h_ref[rs, :] = _rms_norm(x_ref[rs, :], pre_g_ref[...]).astype(BF16)
        items.append(norm)

        def col(off, c):
            return _dot(h_ref[rs, :], w_in_ref[:, off + c * MIX_CHUNK:off + (c + 1) * MIX_CHUNK])

        def q_item(c):
            def f():
                q_ref[rs, c * MIX_CHUNK:(c + 1) * MIX_CHUNK] = (
                    col(OFF_Q, c) * (1.0 / math.sqrt(HEAD_DIM))).astype(BF16)
            return f
        items += [q_item(c) for c in range(ATTN_WIDTH // MIX_CHUNK)]

        def kv_item():
            kv = col(OFF_KV, 0)
            k = kv[:, :KV_WIDTH]
            v = kv[:, KV_WIDTH:]
            k_sw = pltpu.roll(k, HEAD_DIM, 1)
            kd_ref[0, ks, :] = jnp.where(low, k, k_sw).astype(BF16)
            kd_ref[1, ks, :] = jnp.where(low, k_sw, k).astype(BF16)
            vt_ref[:, ks] = v.T.astype(BF16)
        items.append(kv_item)

        def raw_item(dst_ref, dst_rows, off, c):
            def f():
                dst_ref[dst_rows, c * MIX_CHUNK:(c + 1) * MIX_CHUNK] = col(off, c)
            return f
        items += [raw_item(u_ref, rs, OFF_U, c) for c in range(n_chunk)]
        items += [raw_item(gv_ref, slice(0, MIX_SUB), OFF_V, c) for c in range(n_chunk)]

        def layer_norm():
            gv = _gelu(gv_ref[...])
            gc = gv - jnp.mean(gv, axis=-1, keepdims=True)
            vn = gc * lax.rsqrt(jnp.mean(gc * gc, axis=-1, keepdims=True) + LN_EPS)
            vn_ref[rs, :] = (vn * ln_g_ref[...] + ln_b_ref[...]).astype(BF16)
        items.append(layer_norm)
        items += [raw_item(ga_ref, rs, OFF_GA, c) for c in range(n_chunk)]
        items += [raw_item(gb_ref, rs, OFF_GB, c) for c in range(n_chunk)]
        return items

    def attn_slots(j):
        state = {}

        def scores(bi, quad, half):
            def f():
                r0 = bi * ATTN_BLOCK
                g = (4 * quad) // Q_PER_KV
                q0 = r0 + half * HALF
                pieces = []
                for pr in (2 * quad, 2 * quad + 1):
                    qp = q_ref[q0:q0 + HALF, pr * LANES:(pr + 1) * LANES]
                    zq = jnp.zeros_like(qp)
                    pieces += [jnp.where(lane_lo_h, qp, zq), jnp.where(lane_lo_h, zq, qp)]
                qcat = jnp.concatenate(pieces, axis=0)
                k0 = r0 + half * HALF
                s = _dot_nt(kd_ref[g, k0:k0 + 3 * HALF, :], qcat)
                a, b, c = s[:HALF], s[HALF:2 * HALF], s[2 * HALF:]
                if half == 0:
                    if bi == 0:
                        a = jnp.where(step > 0, a, MASK_VALUE)
                        b = jnp.where(step > 0, b, MASK_VALUE)
                    sm = jnp.concatenate([jnp.where(cur_ok, c, a), b], axis=0)
                else:
                    if bi == 0:
                        a = jnp.where(step > 0, a, MASK_VALUE)
                    sm = jnp.concatenate([b, jnp.where(cur_ok, c, a)], axis=0)
                h0 = 4 * quad
                sink = jnp.where(head_slot == 0, sinks_ref[layer, h0],
                                 jnp.where(head_slot == 1, sinks_ref[layer, h0 + 1],
                                           jnp.where(head_slot == 2, sinks_ref[layer, h0 + 2],
                                                     sinks_ref[layer, h0 + 3])))
                m = jnp.maximum(jnp.max(sm, axis=0, keepdims=True), sink)
                p = jnp.exp(sm - m)
                den = jnp.sum(p, axis=0, keepdims=True) + jnp.exp(sink - m)
                pb = p.astype(BF16)
                lo, hi = pb[:HALF], pb[HALF:]
                zb = jnp.zeros_like(lo)
                if half == 0:
                    p2 = jnp.concatenate([jnp.where(cur_ok, zb, lo), hi, jnp.where(cur_ok, lo, zb), zb], axis=0)
                else:
                    p2 = jnp.concatenate([zb, jnp.where(cur_ok, zb, hi), lo, jnp.where(cur_ok, hi, zb)], axis=0)
                state[(bi, quad, half)] = (p2, den)
            return f

        def weighted_sum(bi, quad, half):
            def f():
                r0 = bi * ATTN_BLOCK
                g = (4 * quad) // Q_PER_KV
                q0 = r0 + half * HALF
                p2, den = state.pop((bi, quad, half))
                vt = vt_ref[g * HEAD_DIM:(g + 1) * HEAD_DIM, r0:r0 + 2 * ATTN_BLOCK]
                o = _dot(vt, p2) / den
                t = jnp.concatenate([o[:, :LANES], o[:, LANES:]], axis=0).T
                even, odd = t[:HALF], t[HALF:]
                pair_a = jnp.where(lane_lo_h, even, pltpu.roll(odd, HEAD_DIM, 1))
                pair_b = jnp.where(lane_lo_h, pltpu.roll(even, HEAD_DIM, 1), odd)
                ya_ref[q0:q0 + HALF, 2 * quad * LANES:(2 * quad + 1) * LANES] = pair_a.astype(BF16)
                ya_ref[q0:q0 + HALF, (2 * quad + 1) * LANES:(2 * quad + 2) * LANES] = pair_b.astype(BF16)
            return f

        work = [(2 * j + b, quad, half) for b in range(MIX_SUB // ATTN_BLOCK)
                for quad in range(N_Q_HEADS // 4) for half in range(2)]
        nothing = lambda: None
        slots = []
        for t in range(len(work) + ATTN_SKEW):
            first = scores(*work[t]) if t < len(work) else nothing
            second = weighted_sum(*work[t - ATTN_SKEW]) if t >= ATTN_SKEW else nothing
            slots.append((first, second))
        return slots

    def out_items(j):
        r0 = j * MIX_SUB
        rs = slice(r0, r0 + MIX_SUB)
        ra = slice(r0, r0 + SGU_CHUNK)
        rb = slice(r0 + SGU_CHUNK, r0 + 2 * SGU_CHUNK)
        items = []

        def sgu_item(g):
            def f():
                cg = slice(g * SGU_GROUP_CH, (g + 1) * SGU_GROUP_CH)
                rhs = jnp.concatenate([vn_ref[ra, cg], vn_ref[rb, cg]], axis=1)
                sg = _dot(wg_ref[g], rhs)
                bias = bias_ref[g]
                ys_ref[ra, cg] = (_gelu(u_ref[ra, cg]) * (sg[:, :SGU_GROUP_CH] + bias)).astype(BF16)
                ys_ref[rb, cg] = (_gelu(u_ref[rb, cg]) * (sg[:, SGU_GROUP_CH:] + bias)).astype(BF16)
            return f

        def attn_branch_item(c):
            def f():
                cs = slice(c * MIX_CHUNK, (c + 1) * MIX_CHUNK)
                a_ref[:, cs] = jax.nn.sigmoid(ga_ref[rs, cs]) * _dot(ya_ref[rs, :], w_a_ref[:, cs])
            return f

        sgu_per_dot = SGU_GROUPS // n_chunk
        for c in range(n_chunk):
            items.append(attn_branch_item(c))
            items += [sgu_item(g) for g in range(c * sgu_per_dot, (c + 1) * sgu_per_dot)]

        def merge_item(c):
            def f():
                cs = slice(c * MIX_CHUNK, (c + 1) * MIX_CHUNK)
                s = _dot(ys_ref[rs, :], w_s_ref[:, cs])
                mg_ref[:, cs] = (a_ref[:, cs] + jax.nn.sigmoid(gb_ref[rs, cs]) * s).astype(BF16)
            return f
        items += [merge_item(c) for c in range(n_chunk)]

        def proj_item(c):
            def f():
                cs = slice(c * MIX_CHUNK, (c + 1) * MIX_CHUNK)
                y_ref[:, cs] = _dot(mg_ref[...], w_o_ref[:, cs])
            return f
        items += [proj_item(c) for c in range(n_chunk)]

        def finish():
            o_ref[rs, :] = x_ref[rs, :] + _rms_norm(y_ref[...], post_g_ref[...])
        items.append(finish)
        return items

    n_early = len(proj_items(0)) - n_chunk
    for stage in range(n_sub + 2):
        fillers = []
        if 1 <= stage <= n_sub:
            fillers += proj_items(stage - 1)[n_early:]
        if stage < n_sub:
            fillers += proj_items(stage)[:n_early]
        if stage >= 2:
            fillers += out_items(stage - 2)
        if 1 <= stage <= n_sub:
            _interleave(attn_slots(stage - 1), fillers)
        else:
            for f in fillers:
                f()

    kd_ref[:, :ATTN_BLOCK, :] = kd_ref[:, MIX_ROWS:MIX_ROWS + ATTN_BLOCK, :]
    vt_ref[:, :ATTN_BLOCK] = vt_ref[:, MIX_ROWS:MIX_ROWS + ATTN_BLOCK]


def _mixer(x, sinks, pre_g, w_in, ln_g, ln_b, sgu_w, sgu_bt, w_a, w_s, w_o, post_g, layer, cast):
    seq = x.shape[0]
    rows = MIX_ROWS
    steps = seq // rows
    row = lambda i: (i, 0)
    lay2 = lambda i: (layer, 0, 0)
    lay3 = lambda i: (layer, 0, 0, 0)
    whole = lambda i: (0, 0)
    cast_in, cast_out, cast_shapes = _cast_specs(cast, steps)
    out = pl.pallas_call(
        functools.partial(_mixer_kernel, layer=layer, n_cast=len(cast)),
        grid=(steps,),
        in_specs=[
            pl.BlockSpec(memory_space=pltpu.SMEM),
            pl.BlockSpec((rows, D_MODEL), row),
            _resident((None, 1, D_MODEL), lay2),
            _resident((D_MODEL, IN_WIDTH), whole),
            _resident((None, 1, SGU_WIDTH), lay2),
            _resident((None, 1, SGU_WIDTH), lay2),
            _resident((None, SGU_GROUPS, SGU_CHUNK, SGU_CHUNK), lay3),
            _resident((None, SGU_CHUNK, SGU_GROUPS), lay2),
            _resident((ATTN_WIDTH, D_MODEL), whole),
            _resident((SGU_WIDTH, D_MODEL), whole),
            _resident((D_MODEL, D_MODEL), whole),
            _resident((None, 1, D_MODEL), lay2),
        ] + cast_in,
        out_specs=[pl.BlockSpec((rows, D_MODEL), row)] + cast_out,
        out_shape=[jax.ShapeDtypeStruct(x.shape, x.dtype)] + cast_shapes,
        scratch_shapes=[
            pltpu.VMEM((rows, D_MODEL), BF16),
            pltpu.VMEM((rows, ATTN_WIDTH), BF16),
            pltpu.VMEM((N_KV_HEADS, rows + ATTN_BLOCK, LANES), BF16),
            pltpu.VMEM((KV_WIDTH, rows + ATTN_BLOCK), BF16),
            pltpu.VMEM((rows, SGU_WIDTH), F32),
            pltpu.VMEM((MIX_SUB, SGU_WIDTH), F32),
            pltpu.VMEM((rows, SGU_WIDTH), BF16),
            pltpu.VMEM((rows, D_MODEL), F32),
            pltpu.VMEM((rows, D_MODEL), F32),
            pltpu.VMEM((rows, ATTN_WIDTH), BF16),
            pltpu.VMEM((rows, SGU_WIDTH), BF16),
            pltpu.VMEM((MIX_SUB, D_MODEL), F32),
            pltpu.VMEM((MIX_SUB, D_MODEL), BF16),
            pltpu.VMEM((MIX_SUB, D_MODEL), F32),
            pltpu.VMEM((SGU_GROUPS, SGU_CHUNK, SGU_CHUNK), BF16),
            pltpu.VMEM((SGU_GROUPS, SGU_CHUNK, SGU_GROUP_CH), F32),
        ],
        compiler_params=pltpu.CompilerParams(
            dimension_semantics=("arbitrary",), vmem_limit_bytes=VMEM_LIMIT_BYTES),
        name="token_mixer",
    )(sinks, x, pre_g, w_in, ln_g, ln_b, sgu_w, sgu_bt, w_a, w_s, w_o, post_g, *[w for w, _ in cast])
    return out[0], out[1:]


def kernel(x, ffn1_pre_g, ffn1_w1, ffn1_w2, ffn1_post_g, mix_pre_g, w_in, attn_sinks, sgu_ln_g, sgu_ln_b,
           sgu_w, sgu_b, w_attn_branch, w_sgu_branch, w_out, mix_post_g, ffn2_pre_g, ffn2_w1, ffn2_w2,
           ffn2_post_g):
    batch, seq, d = x.shape
    depth = w_in.shape[0]
    assert batch == 1 and d == D_MODEL and seq % MIX_ROWS == 0 and seq % FFN_ROWS == 0
    vec = lambda g: g.reshape(depth, 1, -1)
    sgu_bt = jnp.swapaxes(sgu_b, 1, 2)
    h = x.reshape(seq, d)
    f1w1, f1w2 = ffn1_w1[0].astype(BF16), ffn1_w2[0].astype(BF16)
    for l in range(depth):
        h, (w_in_b, w_a, w_s, w_o, f2w2) = _ffn(
            h, vec(ffn1_pre_g), f1w1, f1w2, vec(ffn1_post_g), l,
            [(w_in, l), (w_attn_branch, l), (w_sgu_branch, l), (w_out, l), (ffn2_w2, l)])
        h, (f2w1,) = _mixer(h, attn_sinks, vec(mix_pre_g), w_in_b, vec(sgu_ln_g), vec(sgu_ln_b), sgu_w, sgu_bt,
                            w_a, w_s, w_o, vec(mix_post_g), l, [(ffn2_w1, l)])
        nxt = [(ffn1_w1, l + 1), (ffn1_w2, l + 1)] if l + 1 < depth else []
        h, nxt_w = _ffn(h, vec(ffn2_pre_g), f2w1, f2w2, vec(ffn2_post_g), l, nxt)
        if nxt:
            f1w1, f1w2 = nxt_w
    return h.reshape(batch, seq, d)
```

```python
import functools
import math

import jax
import jax.numpy as jnp
from jax import lax
from jax.experimental import pallas as pl
from jax.experimental.pallas import tpu as pltpu

D_MODEL = 1024
D_FF = 2816
HEAD_DIM = 64
N_Q_HEADS = 16
N_KV_HEADS = 2
Q_PER_KV = N_Q_HEADS // N_KV_HEADS
ATTN_WIDTH = N_Q_HEADS * HEAD_DIM
KV_WIDTH = N_KV_HEADS * HEAD_DIM
ATTN_BLOCK = 128
PART = 2 * ATTN_BLOCK // Q_PER_KV
SGU_CHUNK = 128
SGU_GROUPS = 8
SGU_GROUP_CH = 128
SGU_WIDTH = SGU_GROUPS * SGU_GROUP_CH
OFF_Q = 0
OFF_KV = ATTN_WIDTH
OFF_U = OFF_KV + 2 * KV_WIDTH
OFF_V = OFF_U + SGU_WIDTH
OFF_GA = OFF_V + SGU_WIDTH
OFF_GB = OFF_GA + D_MODEL
IN_WIDTH = OFF_GB + D_MODEL

RMS_EPS = 1e-6
LN_EPS = 1e-5
MASK_VALUE = -1e30

LANES = 128
BF16_SUBLANES = 16
MXU_COLS = 256
VMEM_LIMIT_BYTES = 58 * 1024 * 1024

FFN_ROWS = 1024
FFN_SUB = 256
FFN_CHUNK = MXU_COLS
MIX_ROWS = 512
MIX_SUB = 2 * ATTN_BLOCK
MIX_CHUNK = MXU_COLS
ATTN_SKEW = 2

F32 = jnp.float32
BF16 = jnp.bfloat16


def _rms_norm(x, g):
    return x * lax.rsqrt(jnp.mean(x * x, axis=-1, keepdims=True) + RMS_EPS) * g


def _gelu(x):
    return 0.5 * x * (1.0 + lax.erf(x * math.sqrt(0.5)))


def _dot(a, b):
    return jnp.dot(a, b, preferred_element_type=F32)


def _dot_nt(a, b):
    return lax.dot_general(a, b, (((1,), (1,)), ((), ())), preferred_element_type=F32)


def _cast_slabs(src_refs, dst_refs):
    for src, dst in zip(src_refs, dst_refs):
        dst[...] = src[...].astype(BF16)


def _ffn_kernel(x_ref, pre_g_ref, w1_ref, w2_ref, post_g_ref, *refs, n_cast):
    cast_src, o_ref, cast_dst = refs[:n_cast], refs[n_cast], refs[n_cast + 1:2 * n_cast + 1]
    h_ref, act_ref, y_ref = refs[2 * n_cast + 1:]
    _cast_slabs(cast_src, cast_dst)
    n_sub = FFN_ROWS // FFN_SUB

    def rows(j):
        return slice(j * FFN_SUB, (j + 1) * FFN_SUB)

    def norm(j):
        h_ref[rows(j), :] = _rms_norm(x_ref[rows(j), :], pre_g_ref[...]).astype(BF16)

    def finish(j):
        o_ref[rows(j), :] = x_ref[rows(j), :] + 0.5 * _rms_norm(y_ref[rows(j), :], post_g_ref[...])

    def down(j):
        y_ref[rows(j), :] = _dot(act_ref[rows(j), :], w2_ref[...])

    norm(0)
    for j in range(n_sub):
        for c in range(D_FF // FFN_CHUNK):
            lo = c * FFN_CHUNK
            g = _dot(h_ref[rows(j), :], w1_ref[:, lo:lo + FFN_CHUNK])
            u = _dot(h_ref[rows(j), :], w1_ref[:, D_FF + lo:D_FF + lo + FFN_CHUNK])
            act_ref[rows(j), lo:lo + FFN_CHUNK] = (jax.nn.silu(g) * u).astype(BF16)
            if c == 0 and j + 1 < n_sub:
                norm(j + 1)
            if c == 1 and j > 0:
                down(j - 1)
            if c == 3 and j > 0:
                finish(j - 1)
    down(n_sub - 1)
    finish(n_sub - 1)


def _resident(block_shape, index_map):
    return pl.BlockSpec(block_shape, index_map, pipeline_mode=pl.Buffered(1))


def _cast_specs(cast, steps):
    in_specs, out_specs, out_shapes = [], [], []
    for w, layer in cast:
        _, r, c = w.shape
        span = 1
        while (r // (steps // span)) % BF16_SUBLANES:
            span *= 2
        slab = r // (steps // span)
        assert slab * (steps // span) == r
        in_specs.append(pl.BlockSpec((None, slab, c), lambda i, layer=layer, span=span: (layer, i // span, 0)))
        out_specs.append(pl.BlockSpec((slab, c), lambda i, span=span: (i // span, 0)))
        out_shapes.append(jax.ShapeDtypeStruct((r, c), BF16))
    return in_specs, out_specs, out_shapes


def _ffn(x, pre_g, w1, w2, post_g, layer, cast):
    seq = x.shape[0]
    steps = seq // FFN_ROWS
    row = lambda i: (i, 0)
    lay2 = lambda i: (layer, 0, 0)
    whole = lambda i: (0, 0)
    cast_in, cast_out, cast_shapes = _cast_specs(cast, steps)
    out = pl.pallas_call(
        functools.partial(_ffn_kernel, n_cast=len(cast)),
        grid=(steps,),
        in_specs=[
            pl.BlockSpec((FFN_ROWS, D_MODEL), row),
            _resident((None, 1, D_MODEL), lay2),
            _resident((D_MODEL, 2 * D_FF), whole),
            _resident((D_FF, D_MODEL), whole),
            _resident((None, 1, D_MODEL), lay2),
        ] + cast_in,
        out_specs=[pl.BlockSpec((FFN_ROWS, D_MODEL), row)] + cast_out,
        out_shape=[jax.ShapeDtypeStruct(x.shape, x.dtype)] + cast_shapes,
        scratch_shapes=[
            pltpu.VMEM((FFN_ROWS, D_MODEL), BF16),
            pltpu.VMEM((FFN_ROWS, D_FF), BF16),
            pltpu.VMEM((FFN_ROWS, D_MODEL), F32),
        ],
        compiler_params=pltpu.CompilerParams(
            dimension_semantics=("arbitrary",), vmem_limit_bytes=VMEM_LIMIT_BYTES),
        name="ffn_half_step",
    )(x, pre_g, w1, w2, post_g, *[w for w, _ in cast])
    return out[0], out[1:]


def _interleave(slots, fillers):
    n = len(slots)
    done = 0
    for t, (first, second) in enumerate(slots):
        first()
        upto = (len(fillers) * (t + 1)) // n
        for f in fillers[done:upto]:
            f()
        done = upto
        second()


def _mixer_kernel(sinks_ref, x_ref, pre_g_ref, w_in_ref, ln_g_ref, ln_b_ref, sgu_w_ref, sgu_bt_ref,
                  w_a_ref, w_s_ref, w_o_ref, post_g_ref, *refs, layer, n_cast):
    cast_src, o_ref, cast_dst = refs[:n_cast], refs[n_cast], refs[n_cast + 1:2 * n_cast + 1]
    (h_ref, q_ref, kd_ref, vt_ref, u_ref, gv_ref, vn_ref, ga_ref, gb_ref, ya_ref, ys_ref,
     a_ref, mg_ref, y_ref, wg_ref, bias_ref) = refs[2 * n_cast + 1:]
    _cast_slabs(cast_src, cast_dst)
    step = pl.program_id(0)
    n_sub = MIX_ROWS // MIX_SUB
    n_chunk = D_MODEL // MIX_CHUNK

    @pl.when(step == 0)
    def _():
        kd_ref[:, :ATTN_BLOCK, :] = jnp.zeros((N_KV_HEADS, ATTN_BLOCK, LANES), BF16)
        vt_ref[:, :ATTN_BLOCK] = jnp.zeros((KV_WIDTH, ATTN_BLOCK), BF16)
        ti = lax.broadcasted_iota(jnp.int32, (SGU_CHUNK, SGU_CHUNK), 0)
        si = lax.broadcasted_iota(jnp.int32, (SGU_CHUNK, SGU_CHUNK), 1)
        for g in range(SGU_GROUPS):
            wg_ref[g] = jnp.where(si <= ti, sgu_w_ref[g], 0.0).astype(BF16)
            bias_ref[g] = jnp.broadcast_to(sgu_bt_ref[:, g:g + 1], (SGU_CHUNK, SGU_GROUP_CH))

    key_l = lax.broadcasted_iota(jnp.int32, (PART, Q_PER_KV * PART), 0)
    qry_c = lax.broadcasted_iota(jnp.int32, (PART, Q_PER_KV * PART), 1)
    cur_ok = key_l <= (qry_c & (PART - 1))
    head_slot = lax.broadcasted_iota(jnp.int32, (1, Q_PER_KV * PART), 1) // PART
    lane_lo_h = lax.broadcasted_iota(jnp.int32, (PART, LANES), 1) < HEAD_DIM
    low = lax.broadcasted_iota(jnp.int32, (MIX_SUB, LANES), 1) < HEAD_DIM

    def proj_items(j):
        r0 = j * MIX_SUB
        rs = slice(r0, r0 + MIX_SUB)
        ks = slice(ATTN_BLOCK + r0, ATTN_BLOCK + r0 + MIX_SUB)
        items = []

        def norm():
            h_ref[rs, :] = _rms_norm(x_ref[rs, :], pre_g_ref[...]).astype(BF16)
        items.append(norm)

        def col(off, c):
            return _dot(h_ref[rs, :], w_in_ref[:, off + c * MIX_CHUNK:off + (c + 1) * MIX_CHUNK])

        def q_item(c):
            def f():
                q_ref[rs, c * MIX_CHUNK:(c + 1) * MIX_CHUNK] = (
                    col(OFF_Q, c) * (1.0 / math.sqrt(HEAD_DIM))).astype(BF16)
            return f
        items += [q_item(c) for c in range(ATTN_WIDTH // MIX_CHUNK)]

        def kv_item():
            kv = col(OFF_KV, 0)
            k = kv[:, :KV_WIDTH]
            v = kv[:, KV_WIDTH:]
            k_sw = pltpu.roll(k, HEAD_DIM, 1)
            kd_ref[0, ks, :] = jnp.where(low, k, k_sw).astype(BF16)
            kd_ref[1, ks, :] = jnp.where(low, k_sw, k).astype(BF16)
            vt_ref[:, ks] = v.T.astype(BF16)
        items.append(kv_item)

        def raw_item(dst_ref, dst_rows, off, c):
            def f():
                dst_ref[dst_rows, c * MIX_CHUNK:(c + 1) * MIX_CHUNK] = col(off, c)
            return f
        items += [raw_item(u_ref, rs, OFF_U, c) for c in range(n_chunk)]
        items += [raw_item(gv_ref, slice(0, MIX_SUB), OFF_V, c) for c in range(n_chunk)]

        def layer_norm():
            gv = _gelu(gv_ref[...])
            gc = gv - jnp.mean(gv, axis=-1, keepdims=True)
            vn = gc * lax.rsqrt(jnp.mean(gc * gc, axis=-1, keepdims=True) + LN_EPS)
            vn_ref[rs, :] = (vn * ln_g_ref[...] + ln_b_ref[...]).astype(BF16)
        items.append(layer_norm)
        items += [raw_item(ga_ref, rs, OFF_GA, c) for c in range(n_chunk)]
        items += [raw_item(gb_ref, rs, OFF_GB, c) for c in range(n_chunk)]
        return items

    def attn_slots(j):
        state = {}

        def scores(bi, g, part):
            def f():
                r0 = bi * ATTN_BLOCK
                q0 = r0 + part * PART
                pieces = []
                for pr in range(g * Q_PER_KV // 2, (g + 1) * Q_PER_KV // 2):
                    qp = q_ref[q0:q0 + PART, pr * LANES:(pr + 1) * LANES]
                    zq = jnp.zeros_like(qp)
                    pieces += [jnp.where(lane_lo_h, qp, zq), jnp.where(lane_lo_h, zq, qp)]
                qcat = jnp.concatenate(pieces, axis=0)
                s = _dot_nt(kd_ref[g, q0:q0 + ATTN_BLOCK + PART, :], qcat)
                n_prev = ATTN_BLOCK - part * PART
                top, bot = s[:PART], s[ATTN_BLOCK:]
                mid = s[PART:ATTN_BLOCK]
                if bi == 0:
                    top = jnp.where(step > 0, top, MASK_VALUE)
                    if n_prev > PART:
                        mid = jnp.concatenate([jnp.where(step > 0, s[PART:n_prev], MASK_VALUE),
                                               s[n_prev:ATTN_BLOCK]], axis=0) if n_prev < ATTN_BLOCK else (
                            jnp.where(step > 0, mid, MASK_VALUE))
                sm = jnp.concatenate([jnp.where(cur_ok, bot, top), mid], axis=0)
                sink = sinks_ref[layer, g * Q_PER_KV + Q_PER_KV - 1]
                for slot in range(Q_PER_KV - 2, -1, -1):
                    sink = jnp.where(head_slot == slot, sinks_ref[layer, g * Q_PER_KV + slot], sink)
                m = jnp.maximum(jnp.max(sm, axis=0, keepdims=True), sink)
                p = jnp.exp(sm - m)
                den = jnp.sum(p, axis=0, keepdims=True) + jnp.exp(sink - m)
                pb = p.astype(BF16)
                first, rest = pb[:PART], pb[PART:]
                zb = jnp.zeros_like(first)
                rows_ = ([zb] * part + [jnp.where(cur_ok, zb, first), rest, jnp.where(cur_ok, first, zb)]
                         + [zb] * (ATTN_BLOCK // PART - 1 - part))
                state[(bi, g, part)] = (jnp.concatenate(rows_, axis=0), den)
            return f

        def weighted_sum(bi, g, part):
            def f():
                r0 = bi * ATTN_BLOCK
                q0 = r0 + part * PART
                p2, den = state.pop((bi, g, part))
                vt = vt_ref[g * HEAD_DIM:(g + 1) * HEAD_DIM, r0:r0 + 2 * ATTN_BLOCK]
                o = _dot(vt, p2) / den
                t = jnp.concatenate([o[:, :LANES], o[:, LANES:]], axis=0).T
                pr0 = g * Q_PER_KV // 2
                for k in range(2):
                    even, odd = t[2 * k * PART:(2 * k + 1) * PART], t[(2 * k + 1) * PART:(2 * k + 2) * PART]
                    pair_a = jnp.where(lane_lo_h, even, pltpu.roll(odd, HEAD_DIM, 1))
                    pair_b = jnp.where(lane_lo_h, pltpu.roll(even, HEAD_DIM, 1), odd)
                    ya_ref[q0:q0 + PART, (pr0 + k) * LANES:(pr0 + k + 1) * LANES] = pair_a.astype(BF16)
                    ya_ref[q0:q0 + PART, (pr0 + k + 2) * LANES:(pr0 + k + 3) * LANES] = pair_b.astype(BF16)
            return f

        work = [(2 * j + b, g, part) for b in range(MIX_SUB // ATTN_BLOCK)
                for g in range(N_KV_HEADS) for part in range(ATTN_BLOCK // PART)]
        nothing = lambda: None
        slots = []
        for t in range(len(work) + ATTN_SKEW):
            first = scores(*work[t]) if t < len(work) else nothing
            second = weighted_sum(*work[t - ATTN_SKEW]) if t >= ATTN_SKEW else nothing
            slots.append((first, second))
        return slots

    def out_items(j):
        r0 = j * MIX_SUB
        rs = slice(r0, r0 + MIX_SUB)
        ra = slice(r0, r0 + SGU_CHUNK)
        rb = slice(r0 + SGU_CHUNK, r0 + 2 * SGU_CHUNK)
        items = []

        def sgu_item(g):
            def f():
                cg = slice(g * SGU_GROUP_CH, (g + 1) * SGU_GROUP_CH)
                rhs = jnp.concatenate([vn_ref[ra, cg], vn_ref[rb, cg]], axis=1)
                sg = _dot(wg_ref[g], rhs)
                bias = bias_ref[g]
                ys_ref[ra, cg] = (_gelu(u_ref[ra, cg]) * (sg[:, :SGU_GROUP_CH] + bias)).astype(BF16)
                ys_ref[rb, cg] = (_gelu(u_ref[rb, cg]) * (sg[:, SGU_GROUP_CH:] + bias)).astype(BF16)
            return f

        def attn_branch_item(c):
            def f():
                cs = slice(c * MIX_CHUNK, (c + 1) * MIX_CHUNK)
                a_ref[:, cs] = jax.nn.sigmoid(ga_ref[rs, cs]) * _dot(ya_ref[rs, :], w_a_ref[:, cs])
            return f

        sgu_per_dot = SGU_GROUPS // n_chunk
        for c in range(n_chunk):
            items.append(attn_branch_item(c))
            items += [sgu_item(g) for g in range(c * sgu_per_dot, (c + 1) * sgu_per_dot)]

        def merge_item(c):
            def f():
                cs = slice(c * MIX_CHUNK, (c + 1) * MIX_CHUNK)
                s = _dot(ys_ref[rs, :], w_s_ref[:, cs])
                mg_ref[:, cs] = (a_ref[:, cs] + jax.nn.sigmoid(gb_ref[rs, cs]) * s).astype(BF16)
            return f
        items += [merge_item(c) for c in range(n_chunk)]

        def proj_item(c):
            def f():
                cs = slice(c * MIX_CHUNK, (c + 1) * MIX_CHUNK)
                y_ref[:, cs] = _dot(mg_ref[...], w_o_ref[:, cs])
            return f
        items += [proj_item(c) for c in range(n_chunk)]

        def finish():
            o_ref[rs, :] = x_ref[rs, :] + _rms_norm(y_ref[...], post_g_ref[...])
        items.append(finish)
        return items

    n_early = len(proj_items(0)) - n_chunk
    for stage in range(n_sub + 2):
        fillers = []
        if 1 <= stage <= n_sub:
            fillers += proj_items(stage - 1)[n_early:]
        if stage < n_sub:
            fillers += proj_items(stage)[:n_early]
        if stage >= 2:
            fillers += out_items(stage - 2)
        if 1 <= stage <= n_sub:
            _interleave(attn_slots(stage - 1), fillers)
        else:
            for f in fillers:
                f()

    kd_ref[:, :ATTN_BLOCK, :] = kd_ref[:, MIX_ROWS:MIX_ROWS + ATTN_BLOCK, :]
    vt_ref[:, :ATTN_BLOCK] = vt_ref[:, MIX_ROWS:MIX_ROWS + ATTN_BLOCK]


def _mixer(x, sinks, pre_g, w_in, ln_g, ln_b, sgu_w, sgu_bt, w_a, w_s, w_o, post_g, layer, cast):
    seq = x.shape[0]
    rows = MIX_ROWS
    steps = seq // rows
    row = lambda i: (i, 0)
    lay2 = lambda i: (layer, 0, 0)
    lay3 = lambda i: (layer, 0, 0, 0)
    whole = lambda i: (0, 0)
    cast_in, cast_out, cast_shapes = _cast_specs(cast, steps)
    out = pl.pallas_call(
        functools.partial(_mixer_kernel, layer=layer, n_cast=len(cast)),
        grid=(steps,),
        in_specs=[
            pl.BlockSpec(memory_space=pltpu.SMEM),
            pl.BlockSpec((rows, D_MODEL), row),
            _resident((None, 1, D_MODEL), lay2),
            _resident((D_MODEL, IN_WIDTH), whole),
            _resident((None, 1, SGU_WIDTH), lay2),
            _resident((None, 1, SGU_WIDTH), lay2),
            _resident((None, SGU_GROUPS, SGU_CHUNK, SGU_CHUNK), lay3),
            _resident((None, SGU_CHUNK, SGU_GROUPS), lay2),
            _resident((ATTN_WIDTH, D_MODEL), whole),
            _resident((SGU_WIDTH, D_MODEL), whole),
            _resident((D_MODEL, D_MODEL), whole),
            _resident((None, 1, D_MODEL), lay2),
        ] + cast_in,
        out_specs=[pl.BlockSpec((rows, D_MODEL), row)] + cast_out,
        out_shape=[jax.ShapeDtypeStruct(x.shape, x.dtype)] + cast_shapes,
        scratch_shapes=[
            pltpu.VMEM((rows, D_MODEL), BF16),
            pltpu.VMEM((rows, ATTN_WIDTH), BF16),
            pltpu.VMEM((N_KV_HEADS, rows + ATTN_BLOCK, LANES), BF16),
            pltpu.VMEM((KV_WIDTH, rows + ATTN_BLOCK), BF16),
            pltpu.VMEM((rows, SGU_WIDTH), F32),
            pltpu.VMEM((MIX_SUB, SGU_WIDTH), F32),
            pltpu.VMEM((rows, SGU_WIDTH), BF16),
            pltpu.VMEM((rows, D_MODEL), F32),
            pltpu.VMEM((rows, D_MODEL), F32),
            pltpu.VMEM((rows, ATTN_WIDTH), BF16),
            pltpu.VMEM((rows, SGU_WIDTH), BF16),
            pltpu.VMEM((MIX_SUB, D_MODEL), F32),
            pltpu.VMEM((MIX_SUB, D_MODEL), BF16),
            pltpu.VMEM((MIX_SUB, D_MODEL), F32),
            pltpu.VMEM((SGU_GROUPS, SGU_CHUNK, SGU_CHUNK), BF16),
            pltpu.VMEM((SGU_GROUPS, SGU_CHUNK, SGU_GROUP_CH), F32),
        ],
        compiler_params=pltpu.CompilerParams(
            dimension_semantics=("arbitrary",), vmem_limit_bytes=VMEM_LIMIT_BYTES),
        name="token_mixer",
    )(sinks, x, pre_g, w_in, ln_g, ln_b, sgu_w, sgu_bt, w_a, w_s, w_o, post_g, *[w for w, _ in cast])
    return out[0], out[1:]


def kernel(x, ffn1_pre_g, ffn1_w1, ffn1_w2, ffn1_post_g, mix_pre_g, w_in, attn_sinks, sgu_ln_g, sgu_ln_b,
           sgu_w, sgu_b, w_attn_branch, w_sgu_branch, w_out, mix_post_g, ffn2_pre_g, ffn2_w1, ffn2_w2,
           ffn2_post_g):
    batch, seq, d = x.shape
    depth = w_in.shape[0]
    assert batch == 1 and d == D_MODEL and seq % MIX_ROWS == 0 and seq % FFN_ROWS == 0
    vec = lambda g: g.reshape(depth, 1, -1)
    sgu_bt = jnp.swapaxes(sgu_b, 1, 2)
    h = x.reshape(seq, d)
    f1w1, f1w2 = ffn1_w1[0].astype(BF16), ffn1_w2[0].astype(BF16)
    for l in range(depth):
        h, (w_in_b, w_a, w_s, w_o, f2w2) = _ffn(
            h, vec(ffn1_pre_g), f1w1, f1w2, vec(ffn1_post_g), l,
            [(w_in, l), (w_attn_branch, l), (w_sgu_branch, l), (w_out, l), (ffn2_w2, l)])
        h, (f2w1,) = _mixer(h, attn_sinks, vec(mix_pre_g), w_in_b, vec(sgu_ln_g), vec(sgu_ln_b), sgu_w, sgu_bt,
                            w_a, w_s, w_o, vec(mix_post_g), l, [(ffn2_w1, l)])
        nxt = [(ffn1_w1, l + 1), (ffn1_w2, l + 1)] if l + 1 < depth else []
        h, nxt_w = _ffn(h, vec(ffn2_pre_g), f2w1, f2w2, vec(ffn2_post_g), l, nxt)
        if nxt:
            f1w1, f1w2 = nxt_w
    return h.reshape(batch, seq, d)
```
